```python
import math
import jax, jax.numpy as jnp
from jax import lax
import numpy as np

D_MODEL = 2048
BATCH = 4
SEQ = 4096
DEPTH = 2

EPS = 1e-6
GLA_HEADS = 4
GLA_DV = D_MODEL // 2 // GLA_HEADS
GLA_DK = GLA_DV // 2
GLA_QK = GLA_HEADS * GLA_DK
GLA_V = GLA_HEADS * GLA_DV
GLA_RANK = 16
GLA_GATE_NORM = 16.0
GLA_CHUNK = 64
GMLP_HEADS = 8
GMLP_WIDTH = D_MODEL // 2
GMLP_DH = GMLP_WIDTH // GMLP_HEADS
GMLP_CHUNK = 128
IN_COLS = 2 * GLA_QK + 2 * GLA_V + GLA_RANK + 2 * GMLP_WIDTH
N_KEYS = 128
N_EXPERTS = N_KEYS * N_KEYS
PEER_HEADS = 8
PEER_TOPK = 16
PEER_DQ = 256
PEER_BLOCK = 64

kernel_name = "hybrid_gla_gmlp_peer_adaln"


def rmsnorm(x, g):
    xf = x.astype(jnp.float32)
    y = xf * lax.rsqrt(jnp.mean(xf * xf, axis=-1, keepdims=True) + EPS)
    return (y * g.astype(jnp.float32)).astype(x.dtype)


def gla_mixer(q, k, v, r, a_low, w_a2, b_a, norm_g):
    B, S, _ = q.shape
    n = S // GLA_CHUNK
    f32 = jnp.float32
    qh = q.reshape(B, S, GLA_HEADS, GLA_DK).astype(f32) * (GLA_DK ** -0.5)
    kh = k.reshape(B, S, GLA_HEADS, GLA_DK).astype(f32)
    vh = v.reshape(B, S, GLA_HEADS, GLA_DV).astype(f32)
    z = (a_low @ w_a2 + b_a).astype(f32)
    log_a = (jax.nn.log_sigmoid(z) / GLA_GATE_NORM).reshape(B, S, GLA_HEADS, GLA_DK)

    def to_chunks(t):
        return t.reshape(B, n, GLA_CHUNK, GLA_HEADS, t.shape[-1]).transpose(1, 0, 3, 2, 4)

    qc, kc, vc, lac = to_chunks(qh), to_chunks(kh), to_chunks(vh), to_chunks(log_a)
    G = jnp.cumsum(lac, axis=3)
    causal = jnp.tril(jnp.ones((GLA_CHUNK, GLA_CHUNK), dtype=bool))

    def step(state, inp):
        qi, ki, vi, Gi = inp
        diff = Gi[:, :, :, None, :] - Gi[:, :, None, :, :]
        decay = jnp.exp(jnp.where(causal[:, :, None], diff, -jnp.inf))
        attn = jnp.einsum('bhid,bhjd,bhijd->bhij', qi, ki, decay)
        o = attn @ vi + jnp.einsum('bhid,bhde->bhie', qi * jnp.exp(Gi), state)
        g_last = Gi[:, :, -1, :]
        k_dec = ki * jnp.exp(g_last[:, :, None, :] - Gi)
        state = jnp.exp(g_last)[..., None] * state + jnp.einsum('bhjd,bhje->bhde', k_dec, vi)
        return state, o

    state0 = jnp.zeros((B, GLA_HEADS, GLA_DK, GLA_DV), f32)
    _, o = lax.scan(step, state0, (qc, kc, vc, G))
    o = o.transpose(1, 0, 3, 2, 4).reshape(B, S, GLA_HEADS, GLA_DV)
    o = rmsnorm(o, norm_g)
    out = o.reshape(B, S, GLA_V) * jax.nn.silu(r.astype(f32))
    return out.astype(r.dtype)


def gmlp_mixer(u, v, vnorm_g, ws, b, out_g):
    B, S, _ = u.shape
    n = S // GMLP_CHUNK
    u = jax.nn.gelu(u).reshape(B, S, GMLP_HEADS, GMLP_DH)
    v = rmsnorm(jax.nn.gelu(v).reshape(B, S, GMLP_HEADS, GMLP_DH), vnorm_g)
    v = v.reshape(B, n, GMLP_CHUNK, GMLP_HEADS, GMLP_DH)
    w = ws * jnp.tril(jnp.ones((GMLP_CHUNK, GMLP_CHUNK), ws.dtype))
    sv = jnp.einsum('hts,bnshd->bnthd', w, v) + b.T[:, :, None]
    y = u * sv.reshape(B, S, GMLP_HEADS, GMLP_DH)
    y = rmsnorm(y, out_g)
    return y.reshape(B, S, GMLP_WIDTH)


def peer(h, wq, k1, k2, pu, pv):
    B, S, D = h.shape
    T = B * S
    hf = h.reshape(T, D)
    q = (hf @ wq).reshape(T, PEER_HEADS, PEER_DQ).astype(jnp.float32)
    half = PEER_DQ // 2
    s1 = jnp.einsum('thd,hkd->thk', q[..., :half], k1.astype(jnp.float32))
    s2 = jnp.einsum('thd,hkd->thk', q[..., half:], k2.astype(jnp.float32))
    v1, i1 = lax.top_k(s1, PEER_TOPK)
    v2, i2 = lax.top_k(s2, PEER_TOPK)
    cand = (v1[..., :, None] + v2[..., None, :]).reshape(T, PEER_HEADS, PEER_TOPK * PEER_TOPK)
    sc, ci = lax.top_k(cand, PEER_TOPK)
    e = (jnp.take_along_axis(i1, ci // PEER_TOPK, axis=-1) * N_KEYS
         + jnp.take_along_axis(i2, ci % PEER_TOPK, axis=-1))
    g = jax.nn.softmax(sc, axis=-1).astype(h.dtype)
    nb = T // PEER_BLOCK

    def block(args):
        hb, eb, gb = args
        ue = pu[eb]
        a = jax.nn.gelu(jnp.einsum('td,thkd->thk', hb, ue)) * gb
        return jnp.einsum('thk,thkd->td', a, pv[eb])

    y = lax.map(block, (hf.reshape(nb, PEER_BLOCK, D),
                        e.reshape(nb, PEER_BLOCK, PEER_HEADS, PEER_TOPK),
                        g.reshape(nb, PEER_BLOCK, PEER_HEADS, PEER_TOPK)))
    return y.reshape(B, S, D)


def setup_inputs(seed: int = 0) -> dict:
    key = jax.random.key(seed)
    ks = jax.random.split(key, 24)
    D = D_MODEL
    nrm = jax.random.normal
    f32 = jnp.float32
    return {
        "x": nrm(ks[0], (BATCH, SEQ, D), f32),
        "c": nrm(ks[1], (BATCH, D), f32),
        "ada_w": nrm(ks[2], (DEPTH, D, 6 * D), f32) * (0.5 * D ** -0.5),
        "ada_b": nrm(ks[3], (DEPTH, 6 * D), f32) * 0.01,
        "norm1_g": 1.0 + 0.01 * nrm(ks[4], (DEPTH, D), f32),
        "w_in": nrm(ks[5], (DEPTH, D, IN_COLS), f32) * D ** -0.5,
        "gla_w_a2": nrm(ks[6], (DEPTH, GLA_RANK, GLA_QK), f32) * GLA_RANK ** -0.5,
        "gla_b_a": nrm(ks[7], (DEPTH, GLA_QK), f32) * 0.1,
        "gla_norm_g": 1.0 + 0.01 * nrm(ks[8], (DEPTH, GLA_HEADS, GLA_DV), f32),
        "gmlp_vnorm_g": 1.0 + 0.01 * nrm(ks[9], (DEPTH, GMLP_HEADS, GMLP_DH), f32),
        "gmlp_ws": nrm(ks[10], (DEPTH, GMLP_HEADS, GMLP_CHUNK, GMLP_CHUNK), f32) * GMLP_CHUNK ** -0.5,
        "gmlp_b": 1.0 + 0.1 * nrm(ks[11], (DEPTH, GMLP_HEADS, GMLP_CHUNK), f32),
        "gmlp_out_g": 1.0 + 0.01 * nrm(ks[12], (DEPTH, GMLP_HEADS, GMLP_DH), f32),
        "w_out": nrm(ks[13], (DEPTH, D, D), f32) * D ** -0.5,
        "norm2_g": 1.0 + 0.01 * nrm(ks[14], (DEPTH, D), f32),
        "peer_wq": nrm(ks[15], (DEPTH, D, PEER_HEADS * PEER_DQ), f32) * D ** -0.5,
        "peer_k1": nrm(ks[16], (DEPTH, PEER_HEADS, N_KEYS, PEER_DQ // 2), f32) * (PEER_DQ // 2) ** -0.5,
        "peer_k2": nrm(ks[17], (DEPTH, PEER_HEADS, N_KEYS, PEER_DQ // 2), f32) * (PEER_DQ // 2) ** -0.5,
        "peer_u": nrm(ks[18], (DEPTH, N_EXPERTS, D), f32) * D ** -0.5,
        "peer_v": nrm(ks[19], (DEPTH, N_EXPERTS, D), f32) * PEER_HEADS ** -0.5,
        "final_g": 1.0 + 0.01 * nrm(ks[20], (D,), f32),
    }


def reference(x, c, ada_w, ada_b, norm1_g, w_in, gla_w_a2, gla_b_a, gla_norm_g,
              gmlp_vnorm_g, gmlp_ws, gmlp_b, gmlp_out_g, w_out, norm2_g,
              peer_wq, peer_k1, peer_k2, peer_u, peer_v, final_g):
    widths = [GLA_QK, GLA_QK, GLA_V, GLA_V, GLA_RANK, GMLP_WIDTH]
    splits = [int(s) for s in np.cumsum(widths)]
    cond = jax.nn.silu(c)
    for l in range(DEPTH):
        mod = (cond @ ada_w[l] + ada_b[l])[:, None, :]
        sh1, sc1, gt1, sh2, sc2, gt2 = jnp.split(mod, 6, axis=-1)
        h = rmsnorm(x, norm1_g[l]) * (1 + sc1) + sh1
        proj = h @ w_in[l]
        q, k, v, r, a_low, u_sp, v_sp = jnp.split(proj, splits, axis=-1)
        y_gla = gla_mixer(q, k, v, r, a_low, gla_w_a2[l], gla_b_a[l], gla_norm_g[l])
        y_gmlp = gmlp_mixer(u_sp, v_sp, gmlp_vnorm_g[l], gmlp_ws[l], gmlp_b[l], gmlp_out_g[l])
        mix = jnp.concatenate([y_gla, y_gmlp], axis=-1) @ w_out[l]
        x = x + gt1 * mix
        h = rmsnorm(x, norm2_g[l]) * (1 + sc2) + sh2
        x = x + gt2 * peer(h, peer_wq[l], peer_k1[l], peer_k2[l], peer_u[l], peer_v[l])
    return rmsnorm(x, final_g)
```

```python
import functools

import jax
import jax.numpy as jnp
from jax import lax
from jax.experimental import pallas as pl
from jax.experimental.pallas import tpu as pltpu

F32 = jnp.float32
BF16 = jnp.bfloat16

D_MODEL = 2048
EPS = 1e-6
GLA_HEADS = 4
GLA_DV = 256
GLA_DK = 128
GLA_QK = GLA_HEADS * GLA_DK
GLA_V = GLA_HEADS * GLA_DV
GLA_RANK = 16
GLA_CHUNK = 64
GMLP_HEADS = 8
GMLP_WIDTH = 1024
GMLP_DH = 128
GMLP_CHUNK = 128
N_KEYS = 128
N_EXPERTS = N_KEYS * N_KEYS
PEER_HEADS = 8
PEER_TOPK = 16
PEER_DQ = 256

LANES = 128
SUBLANES = 8
PROJ_COLS = 5376
COL_U = 3
COL_VSP = 4
COL_A = 40
VMEM_LIMIT = 56 * 1024 * 1024

NEG_INF = float("-inf")


def _cparams(sem):
    return pltpu.CompilerParams(dimension_semantics=sem, vmem_limit_bytes=VMEM_LIMIT)


def _gelu(x):
    c = 0.7978845608028654
    return 0.5 * x * (1.0 + jnp.tanh(c * (x + 0.044715 * (x * x * x))))


def _split3(x):
    hi = x.astype(BF16)
    r1 = x - hi.astype(F32)
    mid = r1.astype(BF16)
    lo = (r1 - mid.astype(F32)).astype(BF16)
    return hi, mid, lo


def _dot(a, b):
    return jnp.dot(a, b, preferred_element_type=F32)


def _dot_nt(a, b):
    return lax.dot_general(a, b, (((1,), (1,)), ((), ())), preferred_element_type=F32)


def _dot_tn(a, b):
    return lax.dot_general(a, b, (((0,), (0,)), ((), ())), preferred_element_type=F32)


def _mod_kernel(c_ref, w_ref, b_ref, o_ref):
    c = c_ref[...]
    cond = c / (1.0 + jnp.exp(-c))
    acc = jnp.zeros(o_ref.shape, F32)
    w = w_ref[...]
    w_parts = _split3(w)
    for cp in _split3(cond):
        for wp in w_parts:
            acc = acc + _dot(cp, wp)
    o_ref[...] = acc + b_ref[...]


def _modulation(c, ada_w, ada_b):
    depth, d, n = ada_w.shape
    b = c.shape[0]
    rows = 16
    cpad = jnp.zeros((rows, d), F32).at[:b].set(c)
    tn = 512
    out = pl.pallas_call(
        _mod_kernel,
        name="adaln_mod",
        grid=(depth, n // tn),
        in_specs=[
            pl.BlockSpec((rows, d), lambda l, j: (0, 0)),
            pl.BlockSpec((None, d, tn), lambda l, j: (l, 0, j)),
            pl.BlockSpec((None, 1, tn), lambda l, j: (l, 0, j)),
        ],
        out_specs=pl.BlockSpec((None, rows, tn), lambda l, j: (l, 0, j)),
        out_shape=jax.ShapeDtypeStruct((depth, rows, n), F32),
        compiler_params=_cparams(("parallel", "parallel")),
    )(cpad, ada_w, ada_b.reshape(depth, 1, n))
    return out[:, :b]


def _inproj_kernel(x_ref, g_ref, sc_ref, sh_ref, w_ref, o_ref, h_scr):
    @pl.when(pl.program_id(1) == 0)
    def _():
        x = x_ref[...]
        ms = jnp.mean(x * x, axis=-1, keepdims=True)
        y = x * lax.rsqrt(ms + EPS) * g_ref[...]
        h_scr[...] = (y * (1.0 + sc_ref[...]) + sh_ref[...]).astype(BF16)

    o_ref[...] = _dot(h_scr[...], w_ref[...])


def _inproj(x, g, sc, sh, w, seq):
    t, d = x.shape
    n = w.shape[1]
    tm = min(512, seq)
    tn = 768
    per_b = seq // tm
    return pl.pallas_call(
        _inproj_kernel,
        name="norm_inproj",
        grid=(t // tm, n // tn),
        in_specs=[
            pl.BlockSpec((tm, d), lambda i, j: (i, 0)),
            pl.BlockSpec((1, d), lambda i, j: (0, 0)),
            pl.BlockSpec((None, 1, d), lambda i, j: (i // per_b, 0, 0)),
            pl.BlockSpec((None, 1, d), lambda i, j: (i // per_b, 0, 0)),
            pl.BlockSpec((d, tn), lambda i, j: (0, j)),
        ],
        out_specs=pl.BlockSpec((tm, tn), lambda i, j: (i, j)),
        out_shape=jax.ShapeDtypeStruct((t, n), F32),
        scratch_shapes=[pltpu.VMEM((tm, d), BF16)],
        compiler_params=_cparams(("parallel", "arbitrary")),
    )(x, g, sc, sh, w)


GLA_LEVELS = (32, 16, 8, 4, 2, 1)


def _gla_kernel(q_ref, k_ref, v_ref, r_ref, a_ref, wa_ref, ba_ref, g_ref, o_ref,
                state_ref, *, n_chunks):
    C = GLA_CHUNK

    @pl.when(pl.program_id(2) == 0)
    def _():
        state_ref[...] = jnp.zeros(state_ref.shape, F32)

    nl = len(GLA_LEVELS) + 1
    ri = lax.broadcasted_iota(jnp.int32, (C, C), 0)
    ci = lax.broadcasted_iota(jnp.int32, (C, C), 1)
    pieces = [ci <= ri]
    masks = [ci == ri]
    for s in GLA_LEVELS:
        blk_r = ri // (2 * s)
        bound = blk_r * (2 * s) + (s - 1)
        pieces.append(ci <= bound)
        masks.append((blk_r == ci // (2 * s)) & (ri % (2 * s) >= s) & (ci % (2 * s) < s))
    prefix = jnp.concatenate([jnp.where(p, 1.0, 0.0).astype(BF16) for p in pieces], axis=0)

    wa = wa_ref[...].astype(BF16)
    ba = ba_ref[...]
    gain = g_ref[...]
    scale = GLA_DK ** -0.5

    def chunk(c, carry):
        off = pl.multiple_of(c * C, C)
        rows = pl.ds(off, C)
        q = q_ref[rows, :] * scale
        k = k_ref[rows, :]
        v = v_ref[rows, :].astype(BF16)
        z = _dot(a_ref[rows, :].astype(BF16), wa) + ba
        la = -(jnp.maximum(-z, 0.0) + jnp.log1p(jnp.exp(-jnp.abs(z)))) * (1.0 / 16.0)
        hi, mid, lo = _split3(la)
        pref = _dot(prefix, hi) + _dot(prefix, mid) + _dot(prefix, lo)
        G = pref[0:C]
        g_last = G[C - 1:C, :]

        attn = jnp.where(masks[0], _dot_nt(q.astype(BF16), k.astype(BF16)), 0.0)
        for l in range(1, nl):
            e = jnp.exp(-jnp.abs(G - pref[l * C:(l + 1) * C]))
            a_l = _dot_nt((q * e).astype(BF16), (k * e).astype(BF16))
            attn = attn + jnp.where(masks[l], a_l, 0.0)

        st = state_ref[...]
        o = _dot(attn.astype(BF16), v) + _dot_nt((q * jnp.exp(G)).astype(BF16), st.astype(BF16))
        k_dec = (k * jnp.exp(g_last - G)).astype(BF16)
        state_ref[...] = st * jnp.exp(g_last) + _dot_tn(v, k_dec)

        y = o * lax.rsqrt(jnp.mean(o * o, axis=-1, keepdims=True) + EPS) * gain
        r = r_ref[rows, :]
        o_ref[rows, :] = (y * (r / (1.0 + jnp.exp(-r)))).astype(o_ref.dtype)
        return carry

    lax.fori_loop(0, n_chunks, chunk, 0)


def _gla(proj, wa2p, ba, norm_g, batch, seq):
    t = proj.shape[0]
    lc = min(512, seq)
    ns = seq // lc
    row = lambda b, h, s: b * ns + s
    return pl.pallas_call(
        functools.partial(_gla_kernel, n_chunks=lc // GLA_CHUNK),
        name="gla",
        grid=(batch, GLA_HEADS, ns),
        in_specs=[
            pl.BlockSpec((lc, GLA_DK), lambda b, h, s: (row(b, h, s), h)),
            pl.BlockSpec((lc, GLA_DK), lambda b, h, s: (row(b, h, s), GLA_HEADS + h)),
            pl.BlockSpec((lc, GLA_DV), lambda b, h, s: (row(b, h, s), GLA_HEADS + h)),
            pl.BlockSpec((lc, GLA_DV), lambda b, h, s: (row(b, h, s), 2 * GLA_HEADS + h)),
            pl.BlockSpec((lc, LANES), lambda b, h, s: (row(b, h, s), COL_A)),
            pl.BlockSpec((LANES, GLA_DK), lambda b, h, s: (0, h)),
            pl.BlockSpec((1, GLA_DK), lambda b, h, s: (0, h)),
            pl.BlockSpec((None, 1, GLA_DV), lambda b, h, s: (h, 0, 0)),
        ],
        out_specs=pl.BlockSpec((lc, GLA_DV), lambda b, h, s: (row(b, h, s), h)),
        out_shape=jax.ShapeDtypeStruct((t, GLA_V), BF16),
        scratch_shapes=[pltpu.VMEM((GLA_DV, GLA_DK), F32)],
        compiler_params=_cparams(("parallel", "parallel", "arbitrary")),
    )(proj, proj, proj, proj, proj, wa2p, ba, norm_g)


def _gmlp_kernel(u_ref, v_ref, ws_ref, bias_ref, vg_ref, og_ref, o_ref):
    for h in range(GMLP_HEADS):
        cols = slice(h * GMLP_DH, (h + 1) * GMLP_DH)
        v = _gelu(v_ref[:, cols])
        v = v * lax.rsqrt(jnp.mean(v * v, axis=-1, keepdims=True) + EPS) * vg_ref[:, cols]
        sv = _dot(ws_ref[h], v.astype(BF16)) + bias_ref[:, cols]
        y = _gelu(u_ref[:, cols]) * sv
        y = y * lax.rsqrt(jnp.mean(y * y, axis=-1, keepdims=True) + EPS) * og_ref[:, cols]
        o_ref[:, cols] = y.astype(o_ref.dtype)


def _gmlp(proj, ws_causal, bias_full, vnorm_g, out_g):
    t = proj.shape[0]
    c = GMLP_CHUNK
    return pl.pallas_call(
        _gmlp_kernel,
        name="gmlp",
        grid=(t // c,),
        in_specs=[
            pl.BlockSpec((c, GMLP_WIDTH), lambda i: (i, COL_U)),
            pl.BlockSpec((c, GMLP_WIDTH), lambda i: (i, COL_VSP)),
            pl.BlockSpec((GMLP_HEADS, c, c), lambda i: (0, 0, 0)),
            pl.BlockSpec((c, GMLP_WIDTH), lambda i: (0, 0)),
            pl.BlockSpec((1, GMLP_WIDTH), lambda i: (0, 0)),
            pl.BlockSpec((1, GMLP_WIDTH), lambda i: (0, 0)),
        ],
        out_specs=pl.BlockSpec((c, GMLP_WIDTH), lambda i: (i, 0)),
        out_shape=jax.ShapeDtypeStruct((t, GMLP_WIDTH), BF16),
        compiler_params=_cparams(("parallel",)),
    )(proj, proj, ws_causal, bias_full, vnorm_g, out_g)


def _outproj_kernel(yg_ref, ym_ref, wg_ref, wm_ref, x_ref, gt_ref, g2_ref, sc_ref, sh_ref,
                    x1_ref, h2_ref):
    mix = _dot(yg_ref[...], wg_ref[...]) + _dot(ym_ref[...], wm_ref[...])
    x1 = x_ref[...] + gt_ref[...] * mix
    x1_ref[...] = x1
    y = x1 * lax.rsqrt(jnp.mean(x1 * x1, axis=-1, keepdims=True) + EPS) * g2_ref[...]
    h2_ref[...] = (y * (1.0 + sc_ref[...]) + sh_ref[...]).astype(BF16)


def _outproj(yg, ym, w_out, x, gt1, g2, sc2, sh2, seq):
    t, d = x.shape
    tm = min(256, seq)
    per_b = seq // tm
    half = d // 2
    bvec = pl.BlockSpec((None, 1, d), lambda i: (i // per_b, 0, 0))
    return pl.pallas_call(
        _outproj_kernel,
        name="outproj",
        grid=(t // tm,),
        in_specs=[
            pl.BlockSpec((tm, half), lambda i: (i, 0)),
            pl.BlockSpec((tm, half), lambda i: (i, 0)),
            pl.BlockSpec((half, d), lambda i: (0, 0)),
            pl.BlockSpec((half, d), lambda i: (1, 0)),
            pl.BlockSpec((tm, d), lambda i: (i, 0)),
            bvec,
            pl.BlockSpec((1, d), lambda i: (0, 0)),
            bvec,
            bvec,
        ],
        out_specs=[pl.BlockSpec((tm, d), lambda i: (i, 0)),
                   pl.BlockSpec((tm, d), lambda i: (i, 0))],
        out_shape=[jax.ShapeDtypeStruct((t, d), F32), jax.ShapeDtypeStruct((t, d), BF16)],
        compiler_params=_cparams(("parallel",)),
    )(yg, ym, w_out, w_out, x, gt1, g2, sc2, sh2)


CAND_COLS = tuple(PEER_TOPK // (r + 1) for r in range(PEER_TOPK))
BIG_IDX = float(1 << 20)


def _top16(s, vals_ref):
    key = lax.broadcasted_iota(jnp.int32, s.shape, 0).astype(F32)
    rank = jnp.full(s.shape, float(PEER_TOPK), F32)
    for r in range(PEER_TOPK):
        m = jnp.max(s, axis=0, keepdims=True)
        first = jnp.min(jnp.where(s == m, key, BIG_IDX), axis=0, keepdims=True)
        sel = key == first
        rank = jnp.where(sel, float(r), rank)
        s = jnp.where(sel, NEG_INF, s)
        vals_ref[r:r + 1, :] = m
    return rank


def _route_kernel(h_ref, wq_ref, k1_ref, k2_ref, rank2_ref, p2_ref, n1_ref, p1_ref,
                  s1_scr, s2_scr, v1_scr, v2_scr, *, lane_tiles):
    q = _dot(h_ref[...], wq_ref[...])
    half = PEER_DQ // 2
    for h in range(PEER_HEADS):
        q1 = q[:, h * PEER_DQ:h * PEER_DQ + half].astype(BF16)
        q2 = q[:, h * PEER_DQ + half:(h + 1) * PEER_DQ].astype(BF16)
        s1_scr[h] = _dot_nt(k1_ref[h], q1)
        s2_scr[h] = _dot_nt(k2_ref[h], q2)

    sub = lax.broadcasted_iota(jnp.int32, (SUBLANES, LANES), 0).astype(F32)

    def body(it, carry):
        h = it // lane_tiles
        lanes = pl.ds(pl.multiple_of((it % lane_tiles) * LANES, LANES), LANES)
        s1 = s1_scr[h, :, lanes]
        s2 = s2_scr[h, :, lanes]
        rank1 = _top16(s1, v1_scr)
        rank2 = _top16(s2, v2_scr)

        v2a = v2_scr[0:8, :]
        v2b = v2_scr[8:16, :]
        cands = [v1_scr[0:1, :] + v2a, v1_scr[0:1, :] + v2b]
        ids = [sub, sub + 8]
        for r in range(1, 8):
            cands.append(jnp.where(sub < CAND_COLS[r], v1_scr[r:r + 1, :] + v2a, NEG_INF))
            ids.append(sub + PEER_TOPK * r)
        cands.append(v1_scr[8:16, :] + v2_scr[0:1, :])
        ids.append((sub + 8) * PEER_TOPK)
        taken = [jnp.zeros((SUBLANES, LANES), F32) for _ in cands]
        mx = v1_scr[0:1, :] + v2_scr[0:1, :]
        zsum = jnp.zeros((1, LANES), F32)
        for _ in range(PEER_TOPK):
            m = functools.reduce(jnp.maximum, cands)
            m = jnp.max(m, axis=0, keepdims=True)
            first = functools.reduce(
                jnp.minimum, [jnp.where(c == m, i, BIG_IDX) for c, i in zip(cands, ids)])
            first = jnp.min(first, axis=0, keepdims=True)
            sels = [i == first for i in ids]
            cands = [jnp.where(s, NEG_INF, c) for s, c in zip(sels, cands)]
            taken = [jnp.where(s, 1.0, t) for s, t in zip(sels, taken)]
            zsum = zsum + jnp.exp(m - mx)

        counts = [jnp.sum(taken[0] + taken[1], axis=0, keepdims=True)]
        for r in range(1, 8):
            counts.append(jnp.sum(taken[r + 1], axis=0, keepdims=True))
        for r in range(8, PEER_TOPK):
            counts.append(taken[9][r - 8:r - 7, :])
        n1 = jnp.zeros(s1.shape, F32)
        for r in range(PEER_TOPK):
            n1 = jnp.where(rank1 == r, counts[r], n1)

        inv_z = 1.0 / zsum
        n1_ref[h, :, lanes] = n1
        p1_ref[h, :, lanes] = jnp.exp(s1 - v1_scr[0:1, :]) * inv_z
        rank2_ref[h, :, lanes] = rank2.astype(rank2_ref.dtype)
        p2_ref[h, :, lanes] = jnp.exp(s2 - v2_scr[0:1, :]).astype(p2_ref.dtype)
        return carry

    lax.fori_loop(0, PEER_HEADS * lane_tiles, body, 0)


def _route(h2, wq, k1, k2):
    t, d = h2.shape
    tm = min(256, t)
    hk = (PEER_HEADS, N_KEYS, tm)
    out_spec = pl.BlockSpec(hk, lambda i: (0, 0, i))
    kspec = pl.BlockSpec((PEER_HEADS, N_KEYS, PEER_DQ // 2), lambda i: (0, 0, 0))
    return pl.pallas_call(
        functools.partial(_route_kernel, lane_tiles=tm // LANES),
        name="peer_route",
        grid=(t // tm,),
        in_specs=[
            pl.BlockSpec((tm, d), lambda i: (i, 0)),
            pl.BlockSpec(wq.shape, lambda i: (0, 0)),
            kspec, kspec,
        ],
        out_specs=[out_spec, out_spec, out_spec, out_spec],
        out_shape=[
            jax.ShapeDtypeStruct((PEER_HEADS, N_KEYS, t), BF16),
            jax.ShapeDtypeStruct((PEER_HEADS, N_KEYS, t), BF16),
            jax.ShapeDtypeStruct((PEER_HEADS, N_KEYS, t), F32),
            jax.ShapeDtypeStruct((PEER_HEADS, N_KEYS, t), F32),
        ],
        scratch_shapes=[
            pltpu.VMEM(hk, F32), pltpu.VMEM(hk, F32),
            pltpu.VMEM((PEER_TOPK, LANES), F32), pltpu.VMEM((PEER_TOPK, LANES), F32),
        ],
        compiler_params=_cparams(("parallel",)),
    )(h2, wq, k1, k2)


def _peer_kernel(h_ref, pu_ref, pvt_ref, rank2_ref, p2_ref, n1_ref, p1_ref, o_ref, *, a_rows):
    j = pl.program_id(1)

    @pl.when(j == 0)
    def _():
        o_ref[...] = jnp.zeros(o_ref.shape, F32)

    scores = _dot_nt(pu_ref[...], h_ref[...])
    parts = []
    for aa in range(a_rows):
        a = j * a_rows + aa
        w = None
        for h in range(PEER_HEADS):
            n1 = n1_ref[h, pl.ds(a, 1), :].astype(BF16)
            p1 = p1_ref[h, pl.ds(a, 1), :].astype(BF16)
            term = jnp.where(rank2_ref[h] < n1, p2_ref[h], jnp.zeros((), BF16)) * p1
            w = term if w is None else w + term
        act = _gelu(scores[aa * N_KEYS:(aa + 1) * N_KEYS, :])
        parts.append((act * w.astype(F32)).astype(BF16))
    p = jnp.concatenate(parts, axis=0)
    o_ref[...] += _dot(pvt_ref[...], p)


def _peer_dense(h2, pu, pvt, rank2, p2, n1, p1):
    t, d = h2.shape
    tm = min(512, t)
    te = 256
    hk = (PEER_HEADS, N_KEYS, tm)
    rspec = pl.BlockSpec(hk, lambda i, j: (0, 0, i))
    return pl.pallas_call(
        functools.partial(_peer_kernel, a_rows=te // N_KEYS),
        name="peer_dense",
        grid=(t // tm, N_EXPERTS // te),
        in_specs=[
            pl.BlockSpec((tm, d), lambda i, j: (i, 0)),
            pl.BlockSpec((te, d), lambda i, j: (j, 0)),
            pl.BlockSpec((d, te), lambda i, j: (0, j)),
            rspec, rspec, rspec, rspec,
        ],
        out_specs=pl.BlockSpec((d, tm), lambda i, j: (0, i)),
        out_shape=jax.ShapeDtypeStruct((d, t), F32),
        compiler_params=_cparams(("parallel", "arbitrary")),
    )(h2, pu, pvt, rank2, p2, n1, p1)


def _resid_kernel(x_ref, yt_ref, gt_ref, g_ref, o_ref, *, final):
    x2 = x_ref[...] + gt_ref[...] * yt_ref[...].T
    if final:
        x2 = x2 * lax.rsqrt(jnp.mean(x2 * x2, axis=-1, keepdims=True) + EPS) * g_ref[...]
    o_ref[...] = x2


def _resid(x1, yt, gt2, final_g, seq, final):
    t, d = x1.shape
    tm = min(256, seq)
    per_b = seq // tm
    return pl.pallas_call(
        functools.partial(_resid_kernel, final=final),
        name="peer_resid",
        grid=(t // tm,),
        in_specs=[
            pl.BlockSpec((tm, d), lambda i: (i, 0)),
            pl.BlockSpec((d, tm), lambda i: (0, i)),
            pl.BlockSpec((None, 1, d), lambda i: (i // per_b, 0, 0)),
            pl.BlockSpec((1, d), lambda i: (0, 0)),
        ],
        out_specs=pl.BlockSpec((tm, d), lambda i: (i, 0)),
        out_shape=jax.ShapeDtypeStruct((t, d), F32),
        compiler_params=_cparams(("parallel",)),
    )(x1, yt, gt2, final_g)


def _layout_w_in(w_in_l):
    d = w_in_l.shape[0]
    o_a = 2 * GLA_QK + 2 * GLA_V
    o_u = o_a + GLA_RANK
    pad = jnp.zeros((d, PROJ_COLS - (w_in_l.shape[1] - GLA_RANK) - GLA_RANK), w_in_l.dtype)
    w = jnp.concatenate([w_in_l[:, :o_a], w_in_l[:, o_u:], w_in_l[:, o_a:o_u], pad], axis=1)
    return w.astype(BF16)


def kernel(x, c, ada_w, ada_b, norm1_g, w_in, gla_w_a2, gla_b_a, gla_norm_g, gmlp_vnorm_g,
           gmlp_ws, gmlp_b, gmlp_out_g, w_out, norm2_g, peer_wq, peer_k1, peer_k2, peer_u,
           peer_v, final_g):
    batch, seq, d = x.shape
    depth = ada_w.shape[0]
    t = batch * seq
    xf = x.reshape(t, d)

    mod = _modulation(c, ada_w, ada_b)
    causal = jnp.tril(jnp.ones((GMLP_CHUNK, GMLP_CHUNK), F32))

    for l in range(depth):
        sh1, sc1, gt1, sh2, sc2, gt2 = [m.reshape(batch, 1, d) for m in jnp.split(mod[l], 6, axis=-1)]

        proj = _inproj(xf, norm1_g[l].reshape(1, d), sc1, sh1, _layout_w_in(w_in[l]), seq)

        wa2p = jnp.zeros((LANES, GLA_QK), F32).at[:GLA_RANK].set(gla_w_a2[l])
        y_gla = _gla(proj, wa2p, gla_b_a[l].reshape(1, GLA_QK),
                     gla_norm_g[l].reshape(GLA_HEADS, 1, GLA_DV), batch, seq)

        bias_full = jnp.repeat(gmlp_b[l].T, GMLP_DH, axis=1)
        y_gmlp = _gmlp(proj, (gmlp_ws[l] * causal).astype(BF16), bias_full,
                       gmlp_vnorm_g[l].reshape(1, GMLP_WIDTH), gmlp_out_g[l].reshape(1, GMLP_WIDTH))

        x1, h2 = _outproj(y_gla, y_gmlp, w_out[l].astype(BF16), xf, gt1,
                          norm2_g[l].reshape(1, d), sc2, sh2, seq)

        rank2, p2, n1, p1 = _route(h2, peer_wq[l].astype(BF16), peer_k1[l].astype(BF16),
                                   peer_k2[l].astype(BF16))
        yt = _peer_dense(h2, peer_u[l].astype(BF16), peer_v[l].T.astype(BF16), rank2, p2, n1, p1)
        xf = _resid(x1, yt, gt2, final_g.reshape(1, d), seq, final=(l == depth - 1))

    return xf.reshape(batch, seq, d)
```

```python
import functools

import jax
import jax.numpy as jnp
from jax import lax
from jax.experimental import pallas as pl
from jax.experimental.pallas import tpu as pltpu

F32 = jnp.float32
BF16 = jnp.bfloat16

D_MODEL = 2048
EPS = 1e-6
GLA_HEADS = 4
GLA_DV = 256
GLA_DK = 128
GLA_QK = GLA_HEADS * GLA_DK
GLA_V = GLA_HEADS * GLA_DV
GLA_RANK = 16
GLA_CHUNK = 64
GMLP_HEADS = 8
GMLP_WIDTH = 1024
GMLP_DH = 128
GMLP_CHUNK = 128
N_KEYS = 128
N_EXPERTS = N_KEYS * N_KEYS
PEER_HEADS = 8
PEER_TOPK = 16
PEER_DQ = 256

LANES = 128
SUBLANES = 8
PROJ_COLS = 5376
COL_U = 3
COL_VSP = 4
COL_A = 40
VMEM_LIMIT = 56 * 1024 * 1024

NEG_INF = float("-inf")


def _cparams(sem):
    return pltpu.CompilerParams(dimension_semantics=sem, vmem_limit_bytes=VMEM_LIMIT)


def _gelu(x):
    c = 0.7978845608028654
    return 0.5 * x * (1.0 + jnp.tanh(c * (x + 0.044715 * (x * x * x))))


def _split3(x):
    hi = x.astype(BF16)
    r1 = x - hi.astype(F32)
    mid = r1.astype(BF16)
    lo = (r1 - mid.astype(F32)).astype(BF16)
    return hi, mid, lo


def _dot(a, b):
    return jnp.dot(a, b, preferred_element_type=F32)


def _dot_nt(a, b):
    return lax.dot_general(a, b, (((1,), (1,)), ((), ())), preferred_element_type=F32)


def _dot_tn(a, b):
    return lax.dot_general(a, b, (((0,), (0,)), ((), ())), preferred_element_type=F32)


def _mod_kernel(c_ref, w_ref, b_ref, o_ref):
    c = c_ref[...]
    cond = c / (1.0 + jnp.exp(-c))
    acc = jnp.zeros(o_ref.shape, F32)
    w = w_ref[...]
    w_parts = _split3(w)
    for cp in _split3(cond):
        for wp in w_parts:
            acc = acc + _dot(cp, wp)
    o_ref[...] = acc + b_ref[...]


def _modulation(c, ada_w, ada_b):
    depth, d, n = ada_w.shape
    b = c.shape[0]
    rows = 16
    cpad = jnp.zeros((rows, d), F32).at[:b].set(c)
    tn = 512
    out = pl.pallas_call(
        _mod_kernel,
        name="adaln_mod",
        grid=(depth, n // tn),
        in_specs=[
            pl.BlockSpec((rows, d), lambda l, j: (0, 0)),
            pl.BlockSpec((None, d, tn), lambda l, j: (l, 0, j)),
            pl.BlockSpec((None, 1, tn), lambda l, j: (l, 0, j)),
        ],
        out_specs=pl.BlockSpec((None, rows, tn), lambda l, j: (l, 0, j)),
        out_shape=jax.ShapeDtypeStruct((depth, rows, n), F32),
        compiler_params=_cparams(("parallel", "parallel")),
    )(cpad, ada_w, ada_b.reshape(depth, 1, n))
    return out[:, :b]


def _inproj_kernel(x_ref, g_ref, sc_ref, sh_ref, w_ref, o_ref, h_scr):
    @pl.when(pl.program_id(1) == 0)
    def _():
        x = x_ref[...]
        ms = jnp.mean(x * x, axis=-1, keepdims=True)
        y = x * lax.rsqrt(ms + EPS) * g_ref[...]
        h_scr[...] = (y * (1.0 + sc_ref[...]) + sh_ref[...]).astype(BF16)

    o_ref[...] = _dot(h_scr[...], w_ref[...])


def _inproj(x, g, sc, sh, w, seq):
    t, d = x.shape
    n = w.shape[1]
    tm = min(1024, seq)
    tn = 768
    per_b = seq // tm
    return pl.pallas_call(
        _inproj_kernel,
        name="norm_inproj",
        grid=(t // tm, n // tn),
        in_specs=[
            pl.BlockSpec((tm, d), lambda i, j: (i, 0)),
            pl.BlockSpec((1, d), lambda i, j: (0, 0)),
            pl.BlockSpec((None, 1, d), lambda i, j: (i // per_b, 0, 0)),
            pl.BlockSpec((None, 1, d), lambda i, j: (i // per_b, 0, 0)),
            pl.BlockSpec((d, tn), lambda i, j: (0, j)),
        ],
        out_specs=pl.BlockSpec((tm, tn), lambda i, j: (i, j)),
        out_shape=jax.ShapeDtypeStruct((t, n), F32),
        scratch_shapes=[pltpu.VMEM((tm, d), BF16)],
        compiler_params=_cparams(("parallel", "arbitrary")),
    )(x, g, sc, sh, w)


GLA_LEVELS = (32, 16, 8, 4, 2, 1)


def _gla_kernel(q_ref, k_ref, v_ref, r_ref, a_ref, wa_ref, ba_ref, g_ref, o_ref,
                state_ref, *, n_chunks):
    C = GLA_CHUNK

    @pl.when(pl.program_id(1) == 0)
    def _():
        state_ref[...] = jnp.zeros(state_ref.shape, F32)

    nl = len(GLA_LEVELS) + 1
    ri = lax.broadcasted_iota(jnp.int32, (C, C), 0)
    ci = lax.broadcasted_iota(jnp.int32, (C, C), 1)
    pieces = [ci <= ri]
    masks = [ci == ri]
    for s in GLA_LEVELS:
        blk_r = ri // (2 * s)
        bound = blk_r * (2 * s) + (s - 1)
        pieces.append(ci <= bound)
        masks.append((blk_r == ci // (2 * s)) & (ri % (2 * s) >= s) & (ci % (2 * s) < s))
    prefix = jnp.concatenate([jnp.where(p, 1.0, 0.0).astype(BF16) for p in pieces], axis=0)

    wa = wa_ref[...].astype(BF16)
    ba = ba_ref[...]
    gain = g_ref[...]
    scale = GLA_DK ** -0.5

    def chunk(c, carry):
        off = pl.multiple_of(c * C, C)
        rows = pl.ds(off, C)
        z = _dot(a_ref[rows, :].astype(BF16), wa) + ba
        la = -(jnp.maximum(-z, 0.0) + jnp.log1p(jnp.exp(-jnp.abs(z)))) * (1.0 / 16.0)
        hi, mid, lo = _split3(la)
        pref_all = _dot(prefix, hi) + _dot(prefix, mid) + _dot(prefix, lo)

        for h in range(GLA_HEADS):
            kcols = slice(h * GLA_DK, (h + 1) * GLA_DK)
            vcols = slice(h * GLA_DV, (h + 1) * GLA_DV)
            pref = pref_all[:, kcols]
            q = q_ref[rows, kcols] * scale
            k = k_ref[rows, kcols]
            v = v_ref[rows, vcols].astype(BF16)
            G = pref[0:C]
            g_last = G[C - 1:C, :]

            attn = jnp.where(masks[0], _dot_nt(q.astype(BF16), k.astype(BF16)), 0.0)
            for l in range(1, nl):
                e = jnp.exp(-jnp.abs(G - pref[l * C:(l + 1) * C]))
                a_l = _dot_nt((q * e).astype(BF16), (k * e).astype(BF16))
                attn = attn + jnp.where(masks[l], a_l, 0.0)

            st = state_ref[h]
            o = (_dot(attn.astype(BF16), v)
                 + _dot_nt((q * jnp.exp(G)).astype(BF16), st.astype(BF16)))
            k_dec = (k * jnp.exp(g_last - G)).astype(BF16)
            state_ref[h] = st * jnp.exp(g_last) + _dot_tn(v, k_dec)

            y = o * lax.rsqrt(jnp.mean(o * o, axis=-1, keepdims=True) + EPS) * gain[:, vcols]
            r = r_ref[rows, vcols]
            o_ref[rows, vcols] = (y * (r / (1.0 + jnp.exp(-r)))).astype(o_ref.dtype)
        return carry

    lax.fori_loop(0, n_chunks, chunk, 0, unroll=2)


def _gla(proj, wa2p, ba, norm_g, batch, seq):
    t = proj.shape[0]
    lc = min(512, seq)
    ns = seq // lc
    row = lambda b, s: b * ns + s
    return pl.pallas_call(
        functools.partial(_gla_kernel, n_chunks=lc // GLA_CHUNK),
        name="gla",
        grid=(batch, ns),
        in_specs=[
            pl.BlockSpec((lc, GLA_QK), lambda b, s: (row(b, s), 0)),
            pl.BlockSpec((lc, GLA_QK), lambda b, s: (row(b, s), 1)),
            pl.BlockSpec((lc, GLA_V), lambda b, s: (row(b, s), 1)),
            pl.BlockSpec((lc, GLA_V), lambda b, s: (row(b, s), 2)),
            pl.BlockSpec((lc, LANES), lambda b, s: (row(b, s), COL_A)),
            pl.BlockSpec((LANES, GLA_QK), lambda b, s: (0, 0)),
            pl.BlockSpec((1, GLA_QK), lambda b, s: (0, 0)),
            pl.BlockSpec((1, GLA_V), lambda b, s: (0, 0)),
        ],
        out_specs=pl.BlockSpec((lc, GLA_V), lambda b, s: (row(b, s), 0)),
        out_shape=jax.ShapeDtypeStruct((t, GLA_V), BF16),
        scratch_shapes=[pltpu.VMEM((GLA_HEADS, GLA_DV, GLA_DK), F32)],
        compiler_params=_cparams(("parallel", "arbitrary")),
    )(proj, proj, proj, proj, proj, wa2p, ba, norm_g)


def _gmlp_kernel(u_ref, v_ref, ws_ref, bias_ref, vg_ref, og_ref, o_ref):
    for h in range(GMLP_HEADS):
        cols = slice(h * GMLP_DH, (h + 1) * GMLP_DH)
        v = _gelu(v_ref[:, cols])
        v = v * lax.rsqrt(jnp.mean(v * v, axis=-1, keepdims=True) + EPS) * vg_ref[:, cols]
        sv = _dot(ws_ref[h], v.astype(BF16)) + bias_ref[:, cols]
        y = _gelu(u_ref[:, cols]) * sv
        y = y * lax.rsqrt(jnp.mean(y * y, axis=-1, keepdims=True) + EPS) * og_ref[:, cols]
        o_ref[:, cols] = y.astype(o_ref.dtype)


def _gmlp(proj, ws_causal, bias_full, vnorm_g, out_g):
    t = proj.shape[0]
    c = GMLP_CHUNK
    return pl.pallas_call(
        _gmlp_kernel,
        name="gmlp",
        grid=(t // c,),
        in_specs=[
            pl.BlockSpec((c, GMLP_WIDTH), lambda i: (i, COL_U)),
            pl.BlockSpec((c, GMLP_WIDTH), lambda i: (i, COL_VSP)),
            pl.BlockSpec((GMLP_HEADS, c, c), lambda i: (0, 0, 0)),
            pl.BlockSpec((c, GMLP_WIDTH), lambda i: (0, 0)),
            pl.BlockSpec((1, GMLP_WIDTH), lambda i: (0, 0)),
            pl.BlockSpec((1, GMLP_WIDTH), lambda i: (0, 0)),
        ],
        out_specs=pl.BlockSpec((c, GMLP_WIDTH), lambda i: (i, 0)),
        out_shape=jax.ShapeDtypeStruct((t, GMLP_WIDTH), BF16),
        compiler_params=_cparams(("parallel",)),
    )(proj, proj, ws_causal, bias_full, vnorm_g, out_g)


def _outproj_kernel(yg_ref, ym_ref, wg_ref, wm_ref, x_ref, gt_ref, g2_ref, sc_ref, sh_ref,
                    x1_ref, h2_ref):
    mix = _dot(yg_ref[...], wg_ref[...]) + _dot(ym_ref[...], wm_ref[...])
    x1 = x_ref[...] + gt_ref[...] * mix
    x1_ref[...] = x1
    y = x1 * lax.rsqrt(jnp.mean(x1 * x1, axis=-1, keepdims=True) + EPS) * g2_ref[...]
    h2_ref[...] = (y * (1.0 + sc_ref[...]) + sh_ref[...]).astype(BF16)


def _outproj(yg, ym, w_out, x, gt1, g2, sc2, sh2, seq):
    t, d = x.shape
    tm = min(256, seq)
    per_b = seq // tm
    half = d // 2
    bvec = pl.BlockSpec((None, 1, d), lambda i: (i // per_b, 0, 0))
    return pl.pallas_call(
        _outproj_kernel,
        name="outproj",
        grid=(t // tm,),
        in_specs=[
            pl.BlockSpec((tm, half), lambda i: (i, 0)),
            pl.BlockSpec((tm, half), lambda i: (i, 0)),
            pl.BlockSpec((half, d), lambda i: (0, 0)),
            pl.BlockSpec((half, d), lambda i: (1, 0)),
            pl.BlockSpec((tm, d), lambda i: (i, 0)),
            bvec,
            pl.BlockSpec((1, d), lambda i: (0, 0)),
            bvec,
            bvec,
        ],
        out_specs=[pl.BlockSpec((tm, d), lambda i: (i, 0)),
                   pl.BlockSpec((tm, d), lambda i: (i, 0))],
        out_shape=[jax.ShapeDtypeStruct((t, d), F32), jax.ShapeDtypeStruct((t, d), BF16)],
        compiler_params=_cparams(("parallel",)),
    )(yg, ym, w_out, w_out, x, gt1, g2, sc2, sh2)


CAND_COLS = tuple(PEER_TOPK // (r + 1) for r in range(PEER_TOPK))
BIG_IDX = float(1 << 20)


def _top16(s, vals_ref, ties, want_rank):
    key = lax.broadcasted_iota(jnp.int32, s.shape, 0).astype(F32)
    rank = jnp.full(s.shape, float(PEER_TOPK), F32) if want_rank else None
    for r in range(PEER_TOPK):
        m = jnp.max(s, axis=0, keepdims=True)
        sel = s == m
        if ties:
            first = jnp.min(jnp.where(sel, key, BIG_IDX), axis=0, keepdims=True)
            sel = key == first
        if want_rank:
            rank = jnp.where(sel, float(r), rank)
        s = jnp.where(sel, NEG_INF, s)
        vals_ref[r:r + 1, :] = m
    marked = jnp.sum(jnp.where(s == NEG_INF, 1.0, 0.0), axis=0, keepdims=True)
    return rank, marked


def _route_tile(s1, s2, v1_scr, v2_scr, ties):
    width = s1.shape[1]
    sub = lax.broadcasted_iota(jnp.int32, (SUBLANES, width), 0).astype(F32)
    rank1, marked1 = _top16(s1, v1_scr, ties, want_rank=ties)
    rank2, marked2 = _top16(s2, v2_scr, ties, want_rank=True)

    v2a = v2_scr[0:8, :]
    v2b = v2_scr[8:16, :]
    cands = [v1_scr[0:1, :] + v2a, v1_scr[0:1, :] + v2b]
    ids = [sub, sub + 8]
    for r in range(1, 8):
        cands.append(jnp.where(sub < CAND_COLS[r], v1_scr[r:r + 1, :] + v2a, NEG_INF))
        ids.append(sub + PEER_TOPK * r)
    cands.append(v1_scr[8:16, :] + v2_scr[0:1, :])
    ids.append((sub + 8) * PEER_TOPK)
    taken = [jnp.zeros((SUBLANES, width), F32) for _ in cands]
    mx = v1_scr[0:1, :] + v2_scr[0:1, :]
    zsum = jnp.zeros((1, width), F32)
    for _ in range(PEER_TOPK):
        m = functools.reduce(jnp.maximum, cands)
        m = jnp.max(m, axis=0, keepdims=True)
        sels = [c == m for c in cands]
        if ties:
            first = functools.reduce(
                jnp.minimum, [jnp.where(s, i, BIG_IDX) for s, i in zip(sels, ids)])
            first = jnp.min(first, axis=0, keepdims=True)
            sels = [i == first for i in ids]
        cands = [jnp.where(s, NEG_INF, c) for s, c in zip(sels, cands)]
        taken = [jnp.where(s, 1.0, t) for s, t in zip(sels, taken)]
        zsum = zsum + jnp.exp(m - mx)

    counts = [jnp.sum(taken[0] + taken[1], axis=0, keepdims=True)]
    for r in range(1, 8):
        counts.append(jnp.sum(taken[r + 1], axis=0, keepdims=True))
    for r in range(8, PEER_TOPK):
        counts.append(taken[9][r - 8:r - 7, :])
    n1 = jnp.zeros(s1.shape, F32)
    for r in range(PEER_TOPK):
        hit = (rank1 == r) if ties else (s1 == v1_scr[r:r + 1, :])
        n1 = jnp.where(hit, counts[r], n1)

    p1 = jnp.exp(s1 - v1_scr[0:1, :]) * (1.0 / zsum)
    p2 = jnp.exp(s2 - v2_scr[0:1, :])
    return n1, p1, rank2, p2, [marked1, marked2, functools.reduce(jnp.add, counts)]


def _route_kernel(h_ref, wq_ref, k1_ref, k2_ref, rank2_ref, p2_ref, n1_ref, p1_ref,
                  s1_scr, s2_scr, v1_scr, v2_scr):
    half = PEER_DQ // 2

    def scores(h):
        q = _dot(h_ref[...], wq_ref[h])
        s1_scr[h] = _dot_nt(k1_ref[h], q[:, :half].astype(BF16))
        s2_scr[h] = _dot_nt(k2_ref[h], q[:, half:].astype(BF16))

    scores(0)

    def body(h, carry):
        s1 = s1_scr[h]
        s2 = s2_scr[h]
        scores(jnp.minimum(h + 1, PEER_HEADS - 1))

        def run(ties):
            n1, p1, rank2, p2, marked = _route_tile(s1, s2, v1_scr, v2_scr, ties)
            n1_ref[h] = n1
            p1_ref[h] = p1
            rank2_ref[h] = rank2.astype(rank2_ref.dtype)
            p2_ref[h] = p2.astype(p2_ref.dtype)
            return marked

        marked = run(ties=False)
        bad = functools.reduce(
            jnp.maximum, [jnp.where(mk == PEER_TOPK, 0.0, 1.0) for mk in marked])

        @pl.when(jnp.max(bad) > 0.0)
        def _():
            run(ties=True)

        return carry

    lax.fori_loop(0, PEER_HEADS, body, 0)


def _route(h2, wq, k1, k2):
    t, d = h2.shape
    tm = min(256, t)
    hk = (PEER_HEADS, N_KEYS, tm)
    out_spec = pl.BlockSpec(hk, lambda i: (0, 0, i))
    kspec = pl.BlockSpec((PEER_HEADS, N_KEYS, PEER_DQ // 2), lambda i: (0, 0, 0))
    wq_heads = wq.reshape(d, PEER_HEADS, PEER_DQ).transpose(1, 0, 2)
    return pl.pallas_call(
        _route_kernel,
        name="peer_route",
        grid=(t // tm,),
        in_specs=[
            pl.BlockSpec((tm, d), lambda i: (i, 0)),
            pl.BlockSpec(wq_heads.shape, lambda i: (0, 0, 0)),
            kspec, kspec,
        ],
        out_specs=[out_spec, out_spec, out_spec, out_spec],
        out_shape=[
            jax.ShapeDtypeStruct((PEER_HEADS, N_KEYS, t), BF16),
            jax.ShapeDtypeStruct((PEER_HEADS, N_KEYS, t), BF16),
            jax.ShapeDtypeStruct((PEER_HEADS, N_KEYS, t), F32),
            jax.ShapeDtypeStruct((PEER_HEADS, N_KEYS, t), F32),
        ],
        scratch_shapes=[
            pltpu.VMEM(hk, F32), pltpu.VMEM(hk, F32),
            pltpu.VMEM((PEER_TOPK, tm), F32), pltpu.VMEM((PEER_TOPK, tm), F32),
        ],
        compiler_params=_cparams(("parallel",)),
    )(h2, wq_heads, k1, k2)


def _peer_kernel(ht_ref, pu_ref, pvt_ref, rank2_ref, p2_ref, n1_ref, p1_ref, o_ref,
                 pa_scr, pb_scr, *, a_rows, n_e):
    j = pl.program_id(1)
    o_rows = o_ref.shape[0] // a_rows

    def activations(k, p_scr):
        rows = pl.ds(pl.multiple_of(k * N_KEYS, N_KEYS), N_KEYS)
        scores = _dot(pu_ref[rows, :], ht_ref[...])
        a = j * a_rows + k
        w = None
        for h in range(PEER_HEADS):
            n1 = n1_ref[h, pl.ds(a, 1), :].astype(BF16)
            p1 = p1_ref[h, pl.ds(a, 1), :].astype(BF16)
            term = jnp.where(rank2_ref[h] < n1, p2_ref[h], jnp.zeros((), BF16)) * p1
            w = term if w is None else w + term
        p_scr[rows, :] = (_gelu(scores) * w.astype(F32)).astype(BF16)

    def accumulate(k, p_scr):
        rows = pl.ds(pl.multiple_of(k * o_rows, o_rows), o_rows)
        o_ref[rows, :] += _dot(pvt_ref[rows, :], p_scr[...])

    def both(p_new, p_old):
        def body(k, carry):
            activations(k, p_new)
            accumulate(k, p_old)
            return carry
        lax.fori_loop(0, a_rows, body, 0, unroll=4)

    @pl.when(j == 0)
    def _():
        o_ref[...] = jnp.zeros(o_ref.shape, F32)
        lax.fori_loop(0, a_rows, lambda k, c: (activations(k, pa_scr), c)[1], 0)

    @pl.when((j > 0) & (j < n_e) & (j % 2 == 0))
    def _():
        both(pa_scr, pb_scr)

    @pl.when((j < n_e) & (j % 2 == 1))
    def _():
        both(pb_scr, pa_scr)

    @pl.when(j == n_e)
    def _():
        p_last = pb_scr if n_e % 2 == 0 else pa_scr
        lax.fori_loop(0, a_rows, lambda k, c: (accumulate(k, p_last), c)[1], 0)


def _peer_dense(ht, pu, pvt, rank2, p2, n1, p1):
    d, t = ht.shape
    tm = min(512, t)
    te = 512
    n_e = N_EXPERTS // te
    hk = (PEER_HEADS, N_KEYS, tm)
    rspec = pl.BlockSpec(hk, lambda i, j: (0, 0, i))
    return pl.pallas_call(
        functools.partial(_peer_kernel, a_rows=te // N_KEYS, n_e=n_e),
        name="peer_dense",
        grid=(t // tm, n_e + 1),
        in_specs=[
            pl.BlockSpec((d, tm), lambda i, j: (0, i)),
            pl.BlockSpec((te, d), lambda i, j: (jnp.minimum(j, n_e - 1), 0)),
            pl.BlockSpec((d, te), lambda i, j: (0, jnp.maximum(j - 1, 0))),
            rspec, rspec, rspec, rspec,
        ],
        out_specs=pl.BlockSpec((d, tm), lambda i, j: (0, i)),
        out_shape=jax.ShapeDtypeStruct((d, t), F32),
        scratch_shapes=[pltpu.VMEM((te, tm), BF16), pltpu.VMEM((te, tm), BF16)],
        compiler_params=_cparams(("parallel", "arbitrary")),
    )(ht, pu, pvt, rank2, p2, n1, p1)


def _resid_kernel(x_ref, yt_ref, gt_ref, g_ref, o_ref, *, final):
    x2 = x_ref[...] + gt_ref[...] * yt_ref[...].T
    if final:
        x2 = x2 * lax.rsqrt(jnp.mean(x2 * x2, axis=-1, keepdims=True) + EPS) * g_ref[...]
    o_ref[...] = x2


def _resid(x1, yt, gt2, final_g, seq, final):
    t, d = x1.shape
    tm = min(256, seq)
    per_b = seq // tm
    return pl.pallas_call(
        functools.partial(_resid_kernel, final=final),
        name="peer_resid",
        grid=(t // tm,),
        in_specs=[
            pl.BlockSpec((tm, d), lambda i: (i, 0)),
            pl.BlockSpec((d, tm), lambda i: (0, i)),
            pl.BlockSpec((None, 1, d), lambda i: (i // per_b, 0, 0)),
            pl.BlockSpec((1, d), lambda i: (0, 0)),
        ],
        out_specs=pl.BlockSpec((tm, d), lambda i: (i, 0)),
        out_shape=jax.ShapeDtypeStruct((t, d), F32),
        compiler_params=_cparams(("parallel",)),
    )(x1, yt, gt2, final_g)


def _layout_w_in(w_in_l):
    d = w_in_l.shape[0]
    o_a = 2 * GLA_QK + 2 * GLA_V
    o_u = o_a + GLA_RANK
    pad = jnp.zeros((d, PROJ_COLS - (w_in_l.shape[1] - GLA_RANK) - GLA_RANK), w_in_l.dtype)
    w = jnp.concatenate([w_in_l[:, :o_a], w_in_l[:, o_u:], w_in_l[:, o_a:o_u], pad], axis=1)
    return w.astype(BF16)


def kernel(x, c, ada_w, ada_b, norm1_g, w_in, gla_w_a2, gla_b_a, gla_norm_g, gmlp_vnorm_g,
           gmlp_ws, gmlp_b, gmlp_out_g, w_out, norm2_g, peer_wq, peer_k1, peer_k2, peer_u,
           peer_v, final_g):
    batch, seq, d = x.shape
    depth = ada_w.shape[0]
    t = batch * seq
    xf = x.reshape(t, d)

    mod = _modulation(c, ada_w, ada_b)
    causal = jnp.tril(jnp.ones((GMLP_CHUNK, GMLP_CHUNK), F32))

    for l in range(depth):
        sh1, sc1, gt1, sh2, sc2, gt2 = [m.reshape(batch, 1, d) for m in jnp.split(mod[l], 6, axis=-1)]

        proj = _inproj(xf, norm1_g[l].reshape(1, d), sc1, sh1, _layout_w_in(w_in[l]), seq)

        wa2p = jnp.zeros((LANES, GLA_QK), F32).at[:GLA_RANK].set(gla_w_a2[l])
        y_gla = _gla(proj, wa2p, gla_b_a[l].reshape(1, GLA_QK),
                     gla_norm_g[l].reshape(1, GLA_V), batch, seq)

        bias_full = jnp.repeat(gmlp_b[l].T, GMLP_DH, axis=1)
        y_gmlp = _gmlp(proj, (gmlp_ws[l] * causal).astype(BF16), bias_full,
                       gmlp_vnorm_g[l].reshape(1, GMLP_WIDTH), gmlp_out_g[l].reshape(1, GMLP_WIDTH))

        x1, h2 = _outproj(y_gla, y_gmlp, w_out[l].astype(BF16), xf, gt1,
                          norm2_g[l].reshape(1, d), sc2, sh2, seq)

        rank2, p2, n1, p1 = _route(h2, peer_wq[l].astype(BF16), peer_k1[l].astype(BF16),
                                   peer_k2[l].astype(BF16))
        yt = _peer_dense(h2.T, peer_u[l].astype(BF16), peer_v[l].T.astype(BF16), rank2, p2, n1, p1)
        xf = _resid(x1, yt, gt2, final_g.reshape(1, d), seq, final=(l == depth - 1))

    return xf.reshape(batch, seq, d)
```

```python
import functools

import jax
import jax.numpy as jnp
from jax import lax
from jax.experimental import pallas as pl
from jax.experimental.pallas import tpu as pltpu

F32 = jnp.float32
BF16 = jnp.bfloat16

D_MODEL = 2048
EPS = 1e-6
GLA_HEADS = 4
GLA_DV = 256
GLA_DK = 128
GLA_QK = GLA_HEADS * GLA_DK
GLA_V = GLA_HEADS * GLA_DV
GLA_RANK = 16
GLA_CHUNK = 64
GMLP_HEADS = 8
GMLP_WIDTH = 1024
GMLP_DH = 128
GMLP_CHUNK = 128
N_KEYS = 128
N_EXPERTS = N_KEYS * N_KEYS
PEER_HEADS = 8
PEER_TOPK = 16
PEER_DQ = 256

LANES = 128
SUBLANES = 8
PROJ_COLS = 5376
COL_U = 3
COL_VSP = 4
COL_A = 40
VMEM_LIMIT = 56 * 1024 * 1024

NEG_INF = float("-inf")


def _cparams(sem):
    return pltpu.CompilerParams(dimension_semantics=sem, vmem_limit_bytes=VMEM_LIMIT)


def _gelu(x):
    c = 0.7978845608028654
    return 0.5 * x * (1.0 + jnp.tanh(c * (x + 0.044715 * (x * x * x))))


def _split3(x):
    hi = x.astype(BF16)
    r1 = x - hi.astype(F32)
    mid = r1.astype(BF16)
    lo = (r1 - mid.astype(F32)).astype(BF16)
    return hi, mid, lo


def _dot(a, b):
    return jnp.dot(a, b, preferred_element_type=F32)


def _dot_nt(a, b):
    return lax.dot_general(a, b, (((1,), (1,)), ((), ())), preferred_element_type=F32)


def _dot_tn(a, b):
    return lax.dot_general(a, b, (((0,), (0,)), ((), ())), preferred_element_type=F32)


def _mod_kernel(c_ref, w_ref, b_ref, o_ref):
    c = c_ref[...]
    cond = c / (1.0 + jnp.exp(-c))
    acc = jnp.zeros(o_ref.shape, F32)
    w = w_ref[...]
    w_parts = _split3(w)
    for cp in _split3(cond):
        for wp in w_parts:
            acc = acc + _dot(cp, wp)
    o_ref[...] = acc + b_ref[...]


def _modulation(c, ada_w, ada_b):
    depth, d, n = ada_w.shape
    b = c.shape[0]
    rows = 16
    cpad = jnp.zeros((rows, d), F32).at[:b].set(c)
    tn = 512
    out = pl.pallas_call(
        _mod_kernel,
        name="adaln_mod",
        grid=(depth, n // tn),
        in_specs=[
            pl.BlockSpec((rows, d), lambda l, j: (0, 0)),
            pl.BlockSpec((None, d, tn), lambda l, j: (l, 0, j)),
            pl.BlockSpec((None, 1, tn), lambda l, j: (l, 0, j)),
        ],
        out_specs=pl.BlockSpec((None, rows, tn), lambda l, j: (l, 0, j)),
        out_shape=jax.ShapeDtypeStruct((depth, rows, n), F32),
        compiler_params=_cparams(("parallel", "parallel")),
    )(cpad, ada_w, ada_b.reshape(depth, 1, n))
    return out[:, :b]


def _inproj_kernel(x_ref, g_ref, sc_ref, sh_ref, w_ref, o_ref, h_scr):
    @pl.when(pl.program_id(1) == 0)
    def _():
        x = x_ref[...]
        ms = jnp.mean(x * x, axis=-1, keepdims=True)
        y = x * lax.rsqrt(ms + EPS) * g_ref[...]
        h_scr[...] = (y * (1.0 + sc_ref[...]) + sh_ref[...]).astype(BF16)

    o_ref[...] = _dot(h_scr[...], w_ref[...])


def _inproj(x, g, sc, sh, w, seq):
    t, d = x.shape
    n = w.shape[1]
    tm = min(1024, seq)
    tn = 768
    per_b = seq // tm
    return pl.pallas_call(
        _inproj_kernel,
        name="norm_inproj",
        grid=(t // tm, n // tn),
        in_specs=[
            pl.BlockSpec((tm, d), lambda i, j: (i, 0)),
            pl.BlockSpec((1, d), lambda i, j: (0, 0)),
            pl.BlockSpec((None, 1, d), lambda i, j: (i // per_b, 0, 0)),
            pl.BlockSpec((None, 1, d), lambda i, j: (i // per_b, 0, 0)),
            pl.BlockSpec((d, tn), lambda i, j: (0, j)),
        ],
        out_specs=pl.BlockSpec((tm, tn), lambda i, j: (i, j)),
        out_shape=jax.ShapeDtypeStruct((t, n), F32),
        scratch_shapes=[pltpu.VMEM((tm, d), BF16)],
        compiler_params=_cparams(("parallel", "arbitrary")),
    )(x, g, sc, sh, w)


GLA_LEVELS = (32, 16, 8, 4, 2, 1)


def _gla_kernel(q_ref, k_ref, v_ref, r_ref, a_ref, wa_ref, ba_ref, g_ref, o_ref,
                state_ref, *, n_chunks):
    C = GLA_CHUNK

    @pl.when(pl.program_id(1) == 0)
    def _():
        state_ref[...] = jnp.zeros(state_ref.shape, F32)

    nl = len(GLA_LEVELS) + 1
    ri = lax.broadcasted_iota(jnp.int32, (C, C), 0)
    ci = lax.broadcasted_iota(jnp.int32, (C, C), 1)
    pieces = [ci <= ri]
    masks = [ci == ri]
    for s in GLA_LEVELS:
        blk_r = ri // (2 * s)
        bound = blk_r * (2 * s) + (s - 1)
        pieces.append(ci <= bound)
        masks.append((blk_r == ci // (2 * s)) & (ri % (2 * s) >= s) & (ci % (2 * s) < s))
    prefix = jnp.concatenate([jnp.where(p, 1.0, 0.0).astype(BF16) for p in pieces], axis=0)

    wa = wa_ref[...].astype(BF16)
    ba = ba_ref[...]
    gain = g_ref[...]
    scale = GLA_DK ** -0.5

    def chunk(c, carry):
        off = pl.multiple_of(c * C, C)
        rows = pl.ds(off, C)
        z = _dot(a_ref[rows, :].astype(BF16), wa) + ba
        la = -(jnp.maximum(-z, 0.0) + jnp.log1p(jnp.exp(-jnp.abs(z)))) * (1.0 / 16.0)
        hi, mid, lo = _split3(la)
        pref_all = _dot(prefix, hi) + _dot(prefix, mid) + _dot(prefix, lo)

        for h in range(GLA_HEADS):
            kcols = slice(h * GLA_DK, (h + 1) * GLA_DK)
            vcols = slice(h * GLA_DV, (h + 1) * GLA_DV)
            pref = pref_all[:, kcols]
            q = q_ref[rows, kcols] * scale
            k = k_ref[rows, kcols]
            v = v_ref[rows, vcols].astype(BF16)
            G = pref[0:C]
            g_last = G[C - 1:C, :]

            attn = jnp.where(masks[0], _dot_nt(q.astype(BF16), k.astype(BF16)), 0.0)
            for l in range(1, nl):
                e = jnp.exp(-jnp.abs(G - pref[l * C:(l + 1) * C]))
                a_l = _dot_nt((q * e).astype(BF16), (k * e).astype(BF16))
                attn = attn + jnp.where(masks[l], a_l, 0.0)

            st = state_ref[h]
            o = (_dot(attn.astype(BF16), v)
                 + _dot_nt((q * jnp.exp(G)).astype(BF16), st.astype(BF16)))
            k_dec = (k * jnp.exp(g_last - G)).astype(BF16)
            state_ref[h] = st * jnp.exp(g_last) + _dot_tn(v, k_dec)

            y = o * lax.rsqrt(jnp.mean(o * o, axis=-1, keepdims=True) + EPS) * gain[:, vcols]
            r = r_ref[rows, vcols]
            o_ref[rows, vcols] = (y * (r / (1.0 + jnp.exp(-r)))).astype(o_ref.dtype)
        return carry

    lax.fori_loop(0, n_chunks, chunk, 0, unroll=2)


def _gla(proj, wa2p, ba, norm_g, batch, seq):
    t = proj.shape[0]
    lc = min(512, seq)
    ns = seq // lc
    row = lambda b, s: b * ns + s
    return pl.pallas_call(
        functools.partial(_gla_kernel, n_chunks=lc // GLA_CHUNK),
        name="gla",
        grid=(batch, ns),
        in_specs=[
            pl.BlockSpec((lc, GLA_QK), lambda b, s: (row(b, s), 0)),
            pl.BlockSpec((lc, GLA_QK), lambda b, s: (row(b, s), 1)),
            pl.BlockSpec((lc, GLA_V), lambda b, s: (row(b, s), 1)),
            pl.BlockSpec((lc, GLA_V), lambda b, s: (row(b, s), 2)),
            pl.BlockSpec((lc, LANES), lambda b, s: (row(b, s), COL_A)),
            pl.BlockSpec((LANES, GLA_QK), lambda b, s: (0, 0)),
            pl.BlockSpec((1, GLA_QK), lambda b, s: (0, 0)),
            pl.BlockSpec((1, GLA_V), lambda b, s: (0, 0)),
        ],
        out_specs=pl.BlockSpec((lc, GLA_V), lambda b, s: (row(b, s), 0)),
        out_shape=jax.ShapeDtypeStruct((t, GLA_V), BF16),
        scratch_shapes=[pltpu.VMEM((GLA_HEADS, GLA_DV, GLA_DK), F32)],
        compiler_params=_cparams(("parallel", "arbitrary")),
    )(proj, proj, proj, proj, proj, wa2p, ba, norm_g)


def _gmlp_kernel(u_ref, v_ref, ws_ref, bias_ref, vg_ref, og_ref, o_ref):
    for h in range(GMLP_HEADS):
        cols = slice(h * GMLP_DH, (h + 1) * GMLP_DH)
        v = _gelu(v_ref[:, cols])
        v = v * lax.rsqrt(jnp.mean(v * v, axis=-1, keepdims=True) + EPS) * vg_ref[:, cols]
        sv = _dot(ws_ref[h], v.astype(BF16)) + bias_ref[:, cols]
        y = _gelu(u_ref[:, cols]) * sv
        y = y * lax.rsqrt(jnp.mean(y * y, axis=-1, keepdims=True) + EPS) * og_ref[:, cols]
        o_ref[:, cols] = y.astype(o_ref.dtype)


def _gmlp(proj, ws_causal, bias_full, vnorm_g, out_g):
    t = proj.shape[0]
    c = GMLP_CHUNK
    return pl.pallas_call(
        _gmlp_kernel,
        name="gmlp",
        grid=(t // c,),
        in_specs=[
            pl.BlockSpec((c, GMLP_WIDTH), lambda i: (i, COL_U)),
            pl.BlockSpec((c, GMLP_WIDTH), lambda i: (i, COL_VSP)),
            pl.BlockSpec((GMLP_HEADS, c, c), lambda i: (0, 0, 0)),
            pl.BlockSpec((c, GMLP_WIDTH), lambda i: (0, 0)),
            pl.BlockSpec((1, GMLP_WIDTH), lambda i: (0, 0)),
            pl.BlockSpec((1, GMLP_WIDTH), lambda i: (0, 0)),
        ],
        out_specs=pl.BlockSpec((c, GMLP_WIDTH), lambda i: (i, 0)),
        out_shape=jax.ShapeDtypeStruct((t, GMLP_WIDTH), BF16),
        compiler_params=_cparams(("parallel",)),
    )(proj, proj, ws_causal, bias_full, vnorm_g, out_g)


def _outproj_kernel(yg_ref, ym_ref, wg_ref, wm_ref, x_ref, gt_ref, g2_ref, sc_ref, sh_ref,
                    x1_ref, h2_ref, h2t_ref):
    mix = _dot(yg_ref[...], wg_ref[...]) + _dot(ym_ref[...], wm_ref[...])
    x1 = x_ref[...] + gt_ref[...] * mix
    x1_ref[...] = x1
    y = x1 * lax.rsqrt(jnp.mean(x1 * x1, axis=-1, keepdims=True) + EPS) * g2_ref[...]
    h2 = y * (1.0 + sc_ref[...]) + sh_ref[...]
    h2_ref[...] = h2.astype(BF16)
    h2t_ref[...] = h2.T.astype(BF16)


def _outproj(yg, ym, w_out, x, gt1, g2, sc2, sh2, seq):
    t, d = x.shape
    tm = min(256, seq)
    per_b = seq // tm
    half = d // 2
    bvec = pl.BlockSpec((None, 1, d), lambda i: (i // per_b, 0, 0))
    return pl.pallas_call(
        _outproj_kernel,
        name="outproj",
        grid=(t // tm,),
        in_specs=[
            pl.BlockSpec((tm, half), lambda i: (i, 0)),
            pl.BlockSpec((tm, half), lambda i: (i, 0)),
            pl.BlockSpec((half, d), lambda i: (0, 0)),
            pl.BlockSpec((half, d), lambda i: (1, 0)),
            pl.BlockSpec((tm, d), lambda i: (i, 0)),
            bvec,
            pl.BlockSpec((1, d), lambda i: (0, 0)),
            bvec,
            bvec,
        ],
        out_specs=[pl.BlockSpec((tm, d), lambda i: (i, 0)),
                   pl.BlockSpec((tm, d), lambda i: (i, 0)),
                   pl.BlockSpec((d, tm), lambda i: (0, i))],
        out_shape=[jax.ShapeDtypeStruct((t, d), F32), jax.ShapeDtypeStruct((t, d), BF16),
                   jax.ShapeDtypeStruct((d, t), BF16)],
        compiler_params=_cparams(("parallel",)),
    )(yg, ym, w_out, w_out, x, gt1, g2, sc2, sh2)


CAND_COLS = tuple(PEER_TOPK // (r + 1) for r in range(PEER_TOPK))
BIG_IDX = float(1 << 20)


def _top16(s, vals_ref, ties, want_rank):
    key = lax.broadcasted_iota(jnp.int32, s.shape, 0).astype(F32)
    rank = jnp.full(s.shape, float(PEER_TOPK), F32) if want_rank else None
    for r in range(PEER_TOPK):
        m = jnp.max(s, axis=0, keepdims=True)
        sel = s == m
        if ties:
            first = jnp.min(jnp.where(sel, key, BIG_IDX), axis=0, keepdims=True)
            sel = key == first
        if want_rank:
            rank = jnp.where(sel, float(r), rank)
        s = jnp.where(sel, NEG_INF, s)
        vals_ref[r:r + 1, :] = m
    marked = jnp.sum(jnp.where(s == NEG_INF, 1.0, 0.0), axis=0, keepdims=True)
    return rank, marked


def _route_tile(s1, s2, v1_scr, v2_scr, ties):
    width = s1.shape[1]
    sub = lax.broadcasted_iota(jnp.int32, (SUBLANES, width), 0).astype(F32)
    rank1, marked1 = _top16(s1, v1_scr, ties, want_rank=ties)
    rank2, marked2 = _top16(s2, v2_scr, ties, want_rank=True)

    v2a = v2_scr[0:8, :]
    v2b = v2_scr[8:16, :]
    cands = [v1_scr[0:1, :] + v2a, v1_scr[0:1, :] + v2b]
    ids = [sub, sub + 8]
    for r in range(1, 8):
        cands.append(jnp.where(sub < CAND_COLS[r], v1_scr[r:r + 1, :] + v2a, NEG_INF))
        ids.append(sub + PEER_TOPK * r)
    cands.append(v1_scr[8:16, :] + v2_scr[0:1, :])
    ids.append((sub + 8) * PEER_TOPK)
    taken = [jnp.zeros((SUBLANES, width), F32) for _ in cands]
    mx = v1_scr[0:1, :] + v2_scr[0:1, :]
    zsum = jnp.zeros((1, width), F32)
    for _ in range(PEER_TOPK):
        m = functools.reduce(jnp.maximum, cands)
        m = jnp.max(m, axis=0, keepdims=True)
        sels = [c == m for c in cands]
        if ties:
            first = functools.reduce(
                jnp.minimum, [jnp.where(s, i, BIG_IDX) for s, i in zip(sels, ids)])
            first = jnp.min(first, axis=0, keepdims=True)
            sels = [i == first for i in ids]
        cands = [jnp.where(s, NEG_INF, c) for s, c in zip(sels, cands)]
        taken = [jnp.where(s, 1.0, t) for s, t in zip(sels, taken)]
        zsum = zsum + jnp.exp(m - mx)

    counts = [jnp.sum(taken[0] + taken[1], axis=0, keepdims=True)]
    for r in range(1, 8):
        counts.append(jnp.sum(taken[r + 1], axis=0, keepdims=True))
    for r in range(8, PEER_TOPK):
        counts.append(taken[9][r - 8:r - 7, :])
    n1 = jnp.zeros(s1.shape, F32)
    for r in range(PEER_TOPK):
        hit = (rank1 == r) if ties else (s1 == v1_scr[r:r + 1, :])
        n1 = jnp.where(hit, counts[r], n1)

    p1 = jnp.exp(s1 - v1_scr[0:1, :]) * (1.0 / zsum)
    p2 = jnp.exp(s2 - v2_scr[0:1, :])
    return n1, p1, rank2, p2, [marked1, marked2, functools.reduce(jnp.add, counts)]


def _route_kernel(h_ref, wq_ref, k1_ref, k2_ref, rank2_ref, p2_ref, n1_ref, p1_ref,
                  s1_scr, s2_scr, v1_scr, v2_scr):
    half = PEER_DQ // 2

    def scores(h):
        q = _dot(h_ref[...], wq_ref[h])
        s1_scr[h] = _dot_nt(k1_ref[h], q[:, :half].astype(BF16))
        s2_scr[h] = _dot_nt(k2_ref[h], q[:, half:].astype(BF16))

    scores(0)

    def body(h, carry):
        s1 = s1_scr[h]
        s2 = s2_scr[h]
        scores(jnp.minimum(h + 1, PEER_HEADS - 1))

        def run(ties):
            n1, p1, rank2, p2, marked = _route_tile(s1, s2, v1_scr, v2_scr, ties)
            n1_ref[h] = n1
            p1_ref[h] = p1
            rank2_ref[h] = rank2.astype(rank2_ref.dtype)
            p2_ref[h] = p2.astype(p2_ref.dtype)
            return marked

        marked = run(ties=False)
        bad = functools.reduce(
            jnp.maximum, [jnp.where(mk == PEER_TOPK, 0.0, 1.0) for mk in marked])

        @pl.when(jnp.max(bad) > 0.0)
        def _():
            run(ties=True)

        return carry

    lax.fori_loop(0, PEER_HEADS, body, 0)


def _route(h2, wq, k1, k2):
    t, d = h2.shape
    tm = min(512, t)
    hk = (PEER_HEADS, N_KEYS, tm)
    out_spec = pl.BlockSpec(hk, lambda i: (0, 0, i))
    kspec = pl.BlockSpec((PEER_HEADS, N_KEYS, PEER_DQ // 2), lambda i: (0, 0, 0))
    wq_heads = wq.reshape(d, PEER_HEADS, PEER_DQ).transpose(1, 0, 2)
    return pl.pallas_call(
        _route_kernel,
        name="peer_route",
        grid=(t // tm,),
        in_specs=[
            pl.BlockSpec((tm, d), lambda i: (i, 0)),
            pl.BlockSpec(wq_heads.shape, lambda i: (0, 0, 0)),
            kspec, kspec,
        ],
        out_specs=[out_spec, out_spec, out_spec, out_spec],
        out_shape=[
            jax.ShapeDtypeStruct((PEER_HEADS, N_KEYS, t), BF16),
            jax.ShapeDtypeStruct((PEER_HEADS, N_KEYS, t), BF16),
            jax.ShapeDtypeStruct((PEER_HEADS, N_KEYS, t), F32),
            jax.ShapeDtypeStruct((PEER_HEADS, N_KEYS, t), F32),
        ],
        scratch_shapes=[
            pltpu.VMEM(hk, F32), pltpu.VMEM(hk, F32),
            pltpu.VMEM((PEER_TOPK, tm), F32), pltpu.VMEM((PEER_TOPK, tm), F32),
        ],
        compiler_params=_cparams(("parallel",)),
    )(h2, wq_heads, k1, k2)


def _peer_kernel(ht_ref, pu_ref, pvt_ref, rank2_ref, p2_ref, n1_ref, p1_ref, o_ref,
                 pa_scr, pb_scr, *, a_rows, n_e):
    j = pl.program_id(1)
    o_rows = o_ref.shape[0] // a_rows

    def activations(k, p_scr):
        rows = pl.ds(pl.multiple_of(k * N_KEYS, N_KEYS), N_KEYS)
        scores = _dot(pu_ref[rows, :], ht_ref[...])
        a = j * a_rows + k
        w = None
        for h in range(PEER_HEADS):
            n1 = n1_ref[h, pl.ds(a, 1), :].astype(BF16)
            p1 = p1_ref[h, pl.ds(a, 1), :].astype(BF16)
            term = jnp.where(rank2_ref[h] < n1, p2_ref[h], jnp.zeros((), BF16)) * p1
            w = term if w is None else w + term
        p_scr[rows, :] = (_gelu(scores) * w.astype(F32)).astype(BF16)

    def accumulate(k, p_scr):
        rows = pl.ds(pl.multiple_of(k * o_rows, o_rows), o_rows)
        o_ref[rows, :] += _dot(pvt_ref[rows, :], p_scr[...])

    def both(p_new, p_old):
        def body(k, carry):
            activations(k, p_new)
            accumulate(k, p_old)
            return carry
        lax.fori_loop(0, a_rows, body, 0, unroll=4)

    @pl.when(j == 0)
    def _():
        o_ref[...] = jnp.zeros(o_ref.shape, F32)
        lax.fori_loop(0, a_rows, lambda k, c: (activations(k, pa_scr), c)[1], 0)

    @pl.when((j > 0) & (j < n_e) & (j % 2 == 0))
    def _():
        both(pa_scr, pb_scr)

    @pl.when((j < n_e) & (j % 2 == 1))
    def _():
        both(pb_scr, pa_scr)

    @pl.when(j == n_e)
    def _():
        p_last = pb_scr if n_e % 2 == 0 else pa_scr
        lax.fori_loop(0, a_rows, lambda k, c: (accumulate(k, p_last), c)[1], 0)


def _peer_dense(ht, pu, pvt, rank2, p2, n1, p1):
    d, t = ht.shape
    tm = min(1024, t)
    te = 512
    n_e = N_EXPERTS // te
    hk = (PEER_HEADS, N_KEYS, tm)
    once = pl.Buffered(1)
    rspec = pl.BlockSpec(hk, lambda i, j: (0, 0, i), pipeline_mode=once)
    return pl.pallas_call(
        functools.partial(_peer_kernel, a_rows=te // N_KEYS, n_e=n_e),
        name="peer_dense",
        grid=(t // tm, n_e + 1),
        in_specs=[
            pl.BlockSpec((d, tm), lambda i, j: (0, i), pipeline_mode=once),
            pl.BlockSpec((te, d), lambda i, j: (jnp.minimum(j, n_e - 1), 0)),
            pl.BlockSpec((d, te), lambda i, j: (0, jnp.maximum(j - 1, 0))),
            rspec, rspec, rspec, rspec,
        ],
        out_specs=pl.BlockSpec((d, tm), lambda i, j: (0, i)),
        out_shape=jax.ShapeDtypeStruct((d, t), F32),
        scratch_shapes=[pltpu.VMEM((te, tm), BF16), pltpu.VMEM((te, tm), BF16)],
        compiler_params=_cparams(("parallel", "arbitrary")),
    )(ht, pu, pvt, rank2, p2, n1, p1)


def _resid_kernel(x_ref, yt_ref, gt_ref, g_ref, o_ref, *, final):
    x2 = x_ref[...] + gt_ref[...] * yt_ref[...].T
    if final:
        x2 = x2 * lax.rsqrt(jnp.mean(x2 * x2, axis=-1, keepdims=True) + EPS) * g_ref[...]
    o_ref[...] = x2


def _resid(x1, yt, gt2, final_g, seq, final):
    t, d = x1.shape
    tm = min(256, seq)
    per_b = seq // tm
    return pl.pallas_call(
        functools.partial(_resid_kernel, final=final),
        name="peer_resid",
        grid=(t // tm,),
        in_specs=[
            pl.BlockSpec((tm, d), lambda i: (i, 0)),
            pl.BlockSpec((d, tm), lambda i: (0, i)),
            pl.BlockSpec((None, 1, d), lambda i: (i // per_b, 0, 0)),
            pl.BlockSpec((1, d), lambda i: (0, 0)),
        ],
        out_specs=pl.BlockSpec((tm, d), lambda i: (i, 0)),
        out_shape=jax.ShapeDtypeStruct((t, d), F32),
        compiler_params=_cparams(("parallel",)),
    )(x1, yt, gt2, final_g)


def _layout_w_in(w_in_l):
    d = w_in_l.shape[0]
    o_a = 2 * GLA_QK + 2 * GLA_V
    o_u = o_a + GLA_RANK
    pad = jnp.zeros((d, PROJ_COLS - (w_in_l.shape[1] - GLA_RANK) - GLA_RANK), w_in_l.dtype)
    w = jnp.concatenate([w_in_l[:, :o_a], w_in_l[:, o_u:], w_in_l[:, o_a:o_u], pad], axis=1)
    return w.astype(BF16)


def kernel(x, c, ada_w, ada_b, norm1_g, w_in, gla_w_a2, gla_b_a, gla_norm_g, gmlp_vnorm_g,
           gmlp_ws, gmlp_b, gmlp_out_g, w_out, norm2_g, peer_wq, peer_k1, peer_k2, peer_u,
           peer_v, final_g):
    batch, seq, d = x.shape
    depth = ada_w.shape[0]
    t = batch * seq
    xf = x.reshape(t, d)

    mod = _modulation(c, ada_w, ada_b)
    causal = jnp.tril(jnp.ones((GMLP_CHUNK, GMLP_CHUNK), F32))

    for l in range(depth):
        sh1, sc1, gt1, sh2, sc2, gt2 = [m.reshape(batch, 1, d) for m in jnp.split(mod[l], 6, axis=-1)]

        proj = _inproj(xf, norm1_g[l].reshape(1, d), sc1, sh1, _layout_w_in(w_in[l]), seq)

        wa2p = jnp.zeros((LANES, GLA_QK), F32).at[:GLA_RANK].set(gla_w_a2[l])
        y_gla = _gla(proj, wa2p, gla_b_a[l].reshape(1, GLA_QK),
                     gla_norm_g[l].reshape(1, GLA_V), batch, seq)

        bias_full = jnp.repeat(gmlp_b[l].T, GMLP_DH, axis=1)
        y_gmlp = _gmlp(proj, (gmlp_ws[l] * causal).astype(BF16), bias_full,
                       gmlp_vnorm_g[l].reshape(1, GMLP_WIDTH), gmlp_out_g[l].reshape(1, GMLP_WIDTH))

        x1, h2, h2t = _outproj(y_gla, y_gmlp, w_out[l].astype(BF16), xf, gt1,
                          norm2_g[l].reshape(1, d), sc2, sh2, seq)

        rank2, p2, n1, p1 = _route(h2, peer_wq[l].astype(BF16), peer_k1[l].astype(BF16),
                                   peer_k2[l].astype(BF16))
        yt = _peer_dense(h2t, peer_u[l].astype(BF16), peer_v[l].T.astype(BF16), rank2, p2, n1, p1)
        xf = _resid(x1, yt, gt2, final_g.reshape(1, d), seq, final=(l == depth - 1))

    return xf.reshape(batch, seq, d)
```

```python
import functools

import jax
import jax.numpy as jnp
from jax import lax
from jax.experimental import pallas as pl
from jax.experimental.pallas import tpu as pltpu

F32 = jnp.float32
BF16 = jnp.bfloat16

D_MODEL = 2048
EPS = 1e-6
GLA_HEADS = 4
GLA_DV = 256
GLA_DK = 128
GLA_QK = GLA_HEADS * GLA_DK
GLA_V = GLA_HEADS * GLA_DV
GLA_RANK = 16
GLA_CHUNK = 64
GMLP_HEADS = 8
GMLP_WIDTH = 1024
GMLP_DH = 128
GMLP_CHUNK = 128
N_KEYS = 128
N_EXPERTS = N_KEYS * N_KEYS
PEER_HEADS = 8
PEER_TOPK = 16
PEER_DQ = 256

LANES = 128
SUBLANES = 8
PROJ_COLS = 5376
COL_U = 3
COL_VSP = 4
COL_A = 40
VMEM_LIMIT = 56 * 1024 * 1024

NEG_INF = float("-inf")


def _cparams(sem):
    return pltpu.CompilerParams(dimension_semantics=sem, vmem_limit_bytes=VMEM_LIMIT)


def _gelu(x):
    c = 0.7978845608028654
    return 0.5 * x * (1.0 + jnp.tanh(c * (x + 0.044715 * (x * x * x))))


def _split3(x):
    hi = x.astype(BF16)
    r1 = x - hi.astype(F32)
    mid = r1.astype(BF16)
    lo = (r1 - mid.astype(F32)).astype(BF16)
    return hi, mid, lo


def _dot(a, b):
    return jnp.dot(a, b, preferred_element_type=F32)


def _dot_nt(a, b):
    return lax.dot_general(a, b, (((1,), (1,)), ((), ())), preferred_element_type=F32)


def _dot_tn(a, b):
    return lax.dot_general(a, b, (((0,), (0,)), ((), ())), preferred_element_type=F32)


def _mod_kernel(c_ref, w_ref, b_ref, o_ref):
    c = c_ref[...]
    cond = c / (1.0 + jnp.exp(-c))
    acc = jnp.zeros(o_ref.shape, F32)
    w = w_ref[...]
    w_parts = _split3(w)
    for cp in _split3(cond):
        for wp in w_parts:
            acc = acc + _dot(cp, wp)
    o_ref[...] = acc + b_ref[...]


def _modulation(c, ada_w, ada_b):
    depth, d, n = ada_w.shape
    b = c.shape[0]
    rows = 16
    cpad = jnp.zeros((rows, d), F32).at[:b].set(c)
    tn = 512
    out = pl.pallas_call(
        _mod_kernel,
        name="adaln_mod",
        grid=(depth, n // tn),
        in_specs=[
            pl.BlockSpec((rows, d), lambda l, j: (0, 0)),
            pl.BlockSpec((None, d, tn), lambda l, j: (l, 0, j)),
            pl.BlockSpec((None, 1, tn), lambda l, j: (l, 0, j)),
        ],
        out_specs=pl.BlockSpec((None, rows, tn), lambda l, j: (l, 0, j)),
        out_shape=jax.ShapeDtypeStruct((depth, rows, n), F32),
        compiler_params=_cparams(("parallel", "parallel")),
    )(cpad, ada_w, ada_b.reshape(depth, 1, n))
    return out[:, :b]


def _inproj_kernel(x_ref, g_ref, sc_ref, sh_ref, w_ref, o_ref, h_scr):
    @pl.when(pl.program_id(1) == 0)
    def _():
        x = x_ref[...]
        ms = jnp.mean(x * x, axis=-1, keepdims=True)
        y = x * lax.rsqrt(ms + EPS) * g_ref[...]
        h_scr[...] = (y * (1.0 + sc_ref[...]) + sh_ref[...]).astype(BF16)

    o_ref[...] = _dot(h_scr[...], w_ref[...])


PROJ_TN = 768


def _inproj(x, g, sc, sh, w, seq):
    t, d = x.shape
    n_tiles, _, tn = w.shape
    n = n_tiles * tn
    tm = min(1024, seq)
    per_b = seq // tm
    return pl.pallas_call(
        _inproj_kernel,
        name="norm_inproj",
        grid=(t // tm, n_tiles),
        in_specs=[
            pl.BlockSpec((tm, d), lambda i, j: (i, 0)),
            pl.BlockSpec((1, d), lambda i, j: (0, 0)),
            pl.BlockSpec((None, 1, d), lambda i, j: (i // per_b, 0, 0)),
            pl.BlockSpec((None, 1, d), lambda i, j: (i // per_b, 0, 0)),
            pl.BlockSpec((None, d, tn), lambda i, j: (j, 0, 0)),
        ],
        out_specs=pl.BlockSpec((tm, tn), lambda i, j: (i, j)),
        out_shape=jax.ShapeDtypeStruct((t, n), F32),
        scratch_shapes=[pltpu.VMEM((tm, d), BF16)],
        compiler_params=_cparams(("parallel", "arbitrary")),
    )(x, g, sc, sh, w)


GLA_LEVELS = (32, 16, 8, 4, 2, 1)


def _gla_kernel(q_ref, k_ref, v_ref, r_ref, a_ref, wa_ref, ba_ref, g_ref, o_ref,
                state_ref, *, n_chunks):
    C = GLA_CHUNK

    @pl.when(pl.program_id(1) == 0)
    def _():
        state_ref[...] = jnp.zeros(state_ref.shape, F32)

    nl = len(GLA_LEVELS) + 1
    ri = lax.broadcasted_iota(jnp.int32, (C, C), 0)
    ci = lax.broadcasted_iota(jnp.int32, (C, C), 1)
    pieces = [ci <= ri]
    masks = [ci == ri]
    for s in GLA_LEVELS:
        blk_r = ri // (2 * s)
        bound = blk_r * (2 * s) + (s - 1)
        pieces.append(ci <= bound)
        masks.append((blk_r == ci // (2 * s)) & (ri % (2 * s) >= s) & (ci % (2 * s) < s))
    prefix = jnp.concatenate([jnp.where(p, 1.0, 0.0).astype(BF16) for p in pieces], axis=0)

    wa = wa_ref[...].astype(BF16)
    ba = ba_ref[...]
    gain = g_ref[...]
    scale = GLA_DK ** -0.5

    def chunk(c, carry):
        off = pl.multiple_of(c * C, C)
        rows = pl.ds(off, C)
        z = _dot(a_ref[rows, :].astype(BF16), wa) + ba
        la = -(jnp.maximum(-z, 0.0) + jnp.log1p(jnp.exp(-jnp.abs(z)))) * (1.0 / 16.0)
        hi, mid, lo = _split3(la)
        pref_all = _dot(prefix, hi) + _dot(prefix, mid) + _dot(prefix, lo)

        for h in range(GLA_HEADS):
            kcols = slice(h * GLA_DK, (h + 1) * GLA_DK)
            vcols = slice(h * GLA_DV, (h + 1) * GLA_DV)
            pref = pref_all[:, kcols]
            q = q_ref[rows, kcols] * scale
            k = k_ref[rows, kcols]
            v = v_ref[rows, vcols].astype(BF16)
            G = pref[0:C]
            g_last = G[C - 1:C, :]

            attn = jnp.where(masks[0], _dot_nt(q.astype(BF16), k.astype(BF16)), 0.0)
            for l in range(1, nl):
                e = jnp.exp(-jnp.abs(G - pref[l * C:(l + 1) * C]))
                a_l = _dot_nt((q * e).astype(BF16), (k * e).astype(BF16))
                attn = attn + jnp.where(masks[l], a_l, 0.0)

            st = state_ref[h]
            o = (_dot(attn.astype(BF16), v)
                 + _dot_nt((q * jnp.exp(G)).astype(BF16), st.astype(BF16)))
            k_dec = (k * jnp.exp(g_last - G)).astype(BF16)
            state_ref[h] = st * jnp.exp(g_last) + _dot_tn(v, k_dec)

            y = o * lax.rsqrt(jnp.mean(o * o, axis=-1, keepdims=True) + EPS) * gain[:, vcols]
            r = r_ref[rows, vcols]
            o_ref[rows, vcols] = (y * (r / (1.0 + jnp.exp(-r)))).astype(o_ref.dtype)
        return carry

    lax.fori_loop(0, n_chunks, chunk, 0, unroll=2)


def _gla(proj, wa2p, ba, norm_g, batch, seq):
    t = proj.shape[0]
    lc = min(512, seq)
    ns = seq // lc
    row = lambda b, s: b * ns + s
    return pl.pallas_call(
        functools.partial(_gla_kernel, n_chunks=lc // GLA_CHUNK),
        name="gla",
        grid=(batch, ns),
        in_specs=[
            pl.BlockSpec((lc, GLA_QK), lambda b, s: (row(b, s), 0)),
            pl.BlockSpec((lc, GLA_QK), lambda b, s: (row(b, s), 1)),
            pl.BlockSpec((lc, GLA_V), lambda b, s: (row(b, s), 1)),
            pl.BlockSpec((lc, GLA_V), lambda b, s: (row(b, s), 2)),
            pl.BlockSpec((lc, LANES), lambda b, s: (row(b, s), COL_A)),
            pl.BlockSpec((LANES, GLA_QK), lambda b, s: (0, 0)),
            pl.BlockSpec((1, GLA_QK), lambda b, s: (0, 0)),
            pl.BlockSpec((1, GLA_V), lambda b, s: (0, 0)),
        ],
        out_specs=pl.BlockSpec((lc, GLA_V), lambda b, s: (row(b, s), 0)),
        out_shape=jax.ShapeDtypeStruct((t, GLA_V), BF16),
        scratch_shapes=[pltpu.VMEM((GLA_HEADS, GLA_DV, GLA_DK), F32)],
        compiler_params=_cparams(("parallel", "arbitrary")),
    )(proj, proj, proj, proj, proj, wa2p, ba, norm_g)


def _gmlp_kernel(u_ref, v_ref, ws_ref, bias_ref, vg_ref, og_ref, o_ref):
    for h in range(GMLP_HEADS):
        cols = slice(h * GMLP_DH, (h + 1) * GMLP_DH)
        v = _gelu(v_ref[:, cols])
        v = v * lax.rsqrt(jnp.mean(v * v, axis=-1, keepdims=True) + EPS) * vg_ref[:, cols]
        sv = _dot(ws_ref[h], v.astype(BF16)) + bias_ref[:, cols]
        y = _gelu(u_ref[:, cols]) * sv
        y = y * lax.rsqrt(jnp.mean(y * y, axis=-1, keepdims=True) + EPS) * og_ref[:, cols]
        o_ref[:, cols] = y.astype(o_ref.dtype)


def _gmlp(proj, ws_causal, bias_full, vnorm_g, out_g):
    t = proj.shape[0]
    c = GMLP_CHUNK
    return pl.pallas_call(
        _gmlp_kernel,
        name="gmlp",
        grid=(t // c,),
        in_specs=[
            pl.BlockSpec((c, GMLP_WIDTH), lambda i: (i, COL_U)),
            pl.BlockSpec((c, GMLP_WIDTH), lambda i: (i, COL_VSP)),
            pl.BlockSpec((GMLP_HEADS, c, c), lambda i: (0, 0, 0)),
            pl.BlockSpec((c, GMLP_WIDTH), lambda i: (0, 0)),
            pl.BlockSpec((1, GMLP_WIDTH), lambda i: (0, 0)),
            pl.BlockSpec((1, GMLP_WIDTH), lambda i: (0, 0)),
        ],
        out_specs=pl.BlockSpec((c, GMLP_WIDTH), lambda i: (i, 0)),
        out_shape=jax.ShapeDtypeStruct((t, GMLP_WIDTH), BF16),
        compiler_params=_cparams(("parallel",)),
    )(proj, proj, ws_causal, bias_full, vnorm_g, out_g)


def _outproj_kernel(yg_ref, ym_ref, wg_ref, wm_ref, x_ref, gt_ref, g2_ref, sc_ref, sh_ref,
                    x1_ref, h2_ref, h2t_ref):
    mix = _dot(yg_ref[...], wg_ref[...]) + _dot(ym_ref[...], wm_ref[...])
    x1 = x_ref[...] + gt_ref[...] * mix
    x1_ref[...] = x1
    y = x1 * lax.rsqrt(jnp.mean(x1 * x1, axis=-1, keepdims=True) + EPS) * g2_ref[...]
    h2 = y * (1.0 + sc_ref[...]) + sh_ref[...]
    h2_ref[...] = h2.astype(BF16)
    h2t_ref[...] = h2.T.astype(BF16)


def _peer_tm(t):
    return min(512, t)


def _outproj(yg, ym, w_out, x, gt1, g2, sc2, sh2, seq):
    t, d = x.shape
    tm = min(256, seq)
    per_b = seq // tm
    half = d // 2
    ptm = _peer_tm(t)
    per_p = ptm // tm
    bvec = pl.BlockSpec((None, 1, d), lambda i: (i // per_b, 0, 0))
    return pl.pallas_call(
        _outproj_kernel,
        name="outproj",
        grid=(t // tm,),
        in_specs=[
            pl.BlockSpec((tm, half), lambda i: (i, 0)),
            pl.BlockSpec((tm, half), lambda i: (i, 0)),
            pl.BlockSpec((half, d), lambda i: (0, 0)),
            pl.BlockSpec((half, d), lambda i: (1, 0)),
            pl.BlockSpec((tm, d), lambda i: (i, 0)),
            bvec,
            pl.BlockSpec((1, d), lambda i: (0, 0)),
            bvec,
            bvec,
        ],
        out_specs=[pl.BlockSpec((tm, d), lambda i: (i, 0)),
                   pl.BlockSpec((tm, d), lambda i: (i, 0)),
                   pl.BlockSpec((None, d, tm), lambda i: (i // per_p, 0, i % per_p))],
        out_shape=[jax.ShapeDtypeStruct((t, d), F32), jax.ShapeDtypeStruct((t, d), BF16),
                   jax.ShapeDtypeStruct((t // ptm, d, ptm), BF16)],
        compiler_params=_cparams(("parallel",)),
    )(yg, ym, w_out, w_out, x, gt1, g2, sc2, sh2)


CAND_COLS = tuple(PEER_TOPK // (r + 1) for r in range(PEER_TOPK))
BIG_IDX = float(1 << 20)


def _top16(s, vals_ref, ties, want_rank):
    key = lax.broadcasted_iota(jnp.int32, s.shape, 0).astype(F32)
    rank = jnp.full(s.shape, float(PEER_TOPK), F32) if want_rank else None
    for r in range(PEER_TOPK):
        m = jnp.max(s, axis=0, keepdims=True)
        sel = s == m
        if ties:
            first = jnp.min(jnp.where(sel, key, BIG_IDX), axis=0, keepdims=True)
            sel = key == first
        if want_rank:
            rank = jnp.where(sel, float(r), rank)
        s = jnp.where(sel, NEG_INF, s)
        vals_ref[r:r + 1, :] = m
    marked = jnp.sum(jnp.where(s == NEG_INF, 1.0, 0.0), axis=0, keepdims=True)
    return rank, marked


def _route_tile(s1, s2, v1_scr, v2_scr, ties):
    width = s1.shape[1]
    sub = lax.broadcasted_iota(jnp.int32, (SUBLANES, width), 0).astype(F32)
    rank1, marked1 = _top16(s1, v1_scr, ties, want_rank=ties)
    rank2, marked2 = _top16(s2, v2_scr, ties, want_rank=True)

    v2a = v2_scr[0:8, :]
    v2b = v2_scr[8:16, :]
    cands = [v1_scr[0:1, :] + v2a, v1_scr[0:1, :] + v2b]
    ids = [sub, sub + 8]
    for r in range(1, 8):
        cands.append(jnp.where(sub < CAND_COLS[r], v1_scr[r:r + 1, :] + v2a, NEG_INF))
        ids.append(sub + PEER_TOPK * r)
    cands.append(v1_scr[8:16, :] + v2_scr[0:1, :])
    ids.append((sub + 8) * PEER_TOPK)
    taken = [jnp.zeros((SUBLANES, width), F32) for _ in cands]
    mx = v1_scr[0:1, :] + v2_scr[0:1, :]
    zsum = jnp.zeros((1, width), F32)
    for _ in range(PEER_TOPK):
        m = functools.reduce(jnp.maximum, cands)
        m = jnp.max(m, axis=0, keepdims=True)
        sels = [c == m for c in cands]
        if ties:
            first = functools.reduce(
                jnp.minimum, [jnp.where(s, i, BIG_IDX) for s, i in zip(sels, ids)])
            first = jnp.min(first, axis=0, keepdims=True)
            sels = [i == first for i in ids]
        cands = [jnp.where(s, NEG_INF, c) for s, c in zip(sels, cands)]
        taken = [jnp.where(s, 1.0, t) for s, t in zip(sels, taken)]
        zsum = zsum + jnp.exp(m - mx)

    counts = [jnp.sum(taken[0] + taken[1], axis=0, keepdims=True)]
    for r in range(1, 8):
        counts.append(jnp.sum(taken[r + 1], axis=0, keepdims=True))
    for r in range(8, PEER_TOPK):
        counts.append(taken[9][r - 8:r - 7, :])
    n1 = jnp.zeros(s1.shape, F32)
    for r in range(PEER_TOPK):
        hit = (rank1 == r) if ties else (s1 == v1_scr[r:r + 1, :])
        n1 = jnp.where(hit, counts[r], n1)

    p1 = jnp.exp(s1 - v1_scr[0:1, :]) * (1.0 / zsum)
    p2 = jnp.exp(s2 - v2_scr[0:1, :])
    return n1, p1, rank2, p2, [marked1, marked2, functools.reduce(jnp.add, counts)]


def _route_kernel(h_ref, wq_ref, k1_ref, k2_ref, rank2_ref, p2_ref, n1_ref, p1_ref,
                  s1_scr, s2_scr, v1_scr, v2_scr):
    half = PEER_DQ // 2

    def scores(h):
        q = _dot(h_ref[...], wq_ref[h])
        s1_scr[h] = _dot_nt(k1_ref[h], q[:, :half].astype(BF16))
        s2_scr[h] = _dot_nt(k2_ref[h], q[:, half:].astype(BF16))

    scores(0)

    def body(h, carry):
        s1 = s1_scr[h]
        s2 = s2_scr[h]
        scores(jnp.minimum(h + 1, PEER_HEADS - 1))

        def run(ties):
            n1, p1, rank2, p2, marked = _route_tile(s1, s2, v1_scr, v2_scr, ties)
            n1_ref[h] = n1
            p1_ref[h] = p1
            rank2_ref[h] = rank2.astype(rank2_ref.dtype)
            p2_ref[h] = p2.astype(p2_ref.dtype)
            return marked

        marked = run(ties=False)
        bad = functools.reduce(
            jnp.maximum, [jnp.where(mk == PEER_TOPK, 0.0, 1.0) for mk in marked])

        @pl.when(jnp.max(bad) > 0.0)
        def _():
            run(ties=True)

        return carry

    lax.fori_loop(0, PEER_HEADS, body, 0)


def _route(h2, wq, k1, k2):
    t, d = h2.shape
    tm = _peer_tm(t)
    hk = (PEER_HEADS, N_KEYS, tm)
    tiled = (t // tm,) + hk
    out_spec = pl.BlockSpec((None,) + hk, lambda i: (i, 0, 0, 0))
    kspec = pl.BlockSpec((PEER_HEADS, N_KEYS, PEER_DQ // 2), lambda i: (0, 0, 0))
    wq_heads = wq.reshape(d, PEER_HEADS, PEER_DQ).transpose(1, 0, 2)
    return pl.pallas_call(
        _route_kernel,
        name="peer_route",
        grid=(t // tm,),
        in_specs=[
            pl.BlockSpec((tm, d), lambda i: (i, 0)),
            pl.BlockSpec(wq_heads.shape, lambda i: (0, 0, 0)),
            kspec, kspec,
        ],
        out_specs=[out_spec, out_spec, out_spec, out_spec],
        out_shape=[
            jax.ShapeDtypeStruct(tiled, BF16),
            jax.ShapeDtypeStruct(tiled, BF16),
            jax.ShapeDtypeStruct(tiled, F32),
            jax.ShapeDtypeStruct(tiled, F32),
        ],
        scratch_shapes=[
            pltpu.VMEM(hk, F32), pltpu.VMEM(hk, F32),
            pltpu.VMEM((PEER_TOPK, tm), F32), pltpu.VMEM((PEER_TOPK, tm), F32),
        ],
        compiler_params=_cparams(("parallel",)),
    )(h2, wq_heads, k1, k2)


def _peer_kernel(ht_ref, pu_ref, pvt_ref, rank2_ref, p2_ref, n1_ref, p1_ref, o_ref,
                 pa_scr, pb_scr, *, a_rows, n_e):
    j = pl.program_id(1)
    group = 1
    n_groups = a_rows // group
    o_rows = o_ref.shape[0] // n_groups

    def activations(k, p_scr):
        base = pl.multiple_of(k * (group * N_KEYS), group * N_KEYS)
        scores = _dot(pu_ref[pl.ds(base, group * N_KEYS), :], ht_ref[...])
        for g in range(group):
            a = j * a_rows + k * group + g
            w = None
            for h in range(PEER_HEADS):
                n1 = n1_ref[h, pl.ds(a, 1), :].astype(BF16)
                p1 = p1_ref[h, pl.ds(a, 1), :].astype(BF16)
                term = jnp.where(rank2_ref[h] < n1, p2_ref[h], jnp.zeros((), BF16)) * p1
                w = term if w is None else w + term
            act = _gelu(scores[g * N_KEYS:(g + 1) * N_KEYS, :])
            p_scr[pl.ds(base + g * N_KEYS, N_KEYS), :] = (act * w.astype(F32)).astype(BF16)

    def accumulate(k, p_scr):
        rows = pl.ds(pl.multiple_of(k * o_rows, o_rows), o_rows)
        o_ref[rows, :] += _dot(pvt_ref[rows, :], p_scr[...])

    def both(p_new, p_old):
        def body(k, carry):
            activations(k, p_new)
            accumulate(k, p_old)
            return carry
        lax.fori_loop(0, n_groups, body, 0, unroll=n_groups)

    @pl.when(j == 0)
    def _():
        o_ref[...] = jnp.zeros(o_ref.shape, F32)
        lax.fori_loop(0, n_groups, lambda k, c: (activations(k, pa_scr), c)[1], 0)

    @pl.when((j > 0) & (j < n_e) & (j % 2 == 0))
    def _():
        both(pa_scr, pb_scr)

    @pl.when((j < n_e) & (j % 2 == 1))
    def _():
        both(pb_scr, pa_scr)

    @pl.when(j == n_e)
    def _():
        p_last = pb_scr if n_e % 2 == 0 else pa_scr
        lax.fori_loop(0, n_groups, lambda k, c: (accumulate(k, p_last), c)[1], 0)


PEER_TE = 512


def _peer_dense(ht, pu, pvt, rank2, p2, n1, p1):
    n_t, d, tm = ht.shape
    te = PEER_TE
    n_e = N_EXPERTS // te
    rspec = pl.BlockSpec((None, PEER_HEADS, N_KEYS, tm), lambda i, j: (i, 0, 0, 0))
    return pl.pallas_call(
        functools.partial(_peer_kernel, a_rows=te // N_KEYS, n_e=n_e),
        name="peer_dense",
        grid=(n_t, n_e + 1),
        in_specs=[
            pl.BlockSpec((None, d, tm), lambda i, j: (i, 0, 0)),
            pl.BlockSpec((te, d), lambda i, j: (jnp.minimum(j, n_e - 1), 0)),
            pl.BlockSpec((None, d, te), lambda i, j: (jnp.maximum(j - 1, 0), 0, 0)),
            rspec, rspec, rspec, rspec,
        ],
        out_specs=pl.BlockSpec((None, d, tm), lambda i, j: (i, 0, 0)),
        out_shape=jax.ShapeDtypeStruct((n_t, d, tm), F32),
        scratch_shapes=[pltpu.VMEM((te, tm), BF16), pltpu.VMEM((te, tm), BF16)],
        compiler_params=_cparams(("parallel", "arbitrary")),
    )(ht, pu, pvt, rank2, p2, n1, p1)


def _resid_kernel(x_ref, yt_ref, gt_ref, g_ref, o_ref, *, final):
    x2 = x_ref[...] + gt_ref[...] * yt_ref[...].T
    if final:
        x2 = x2 * lax.rsqrt(jnp.mean(x2 * x2, axis=-1, keepdims=True) + EPS) * g_ref[...]
    o_ref[...] = x2


def _resid(x1, yt, gt2, final_g, seq, final):
    t, d = x1.shape
    tm = min(256, seq)
    per_b = seq // tm
    per_p = yt.shape[2] // tm
    return pl.pallas_call(
        functools.partial(_resid_kernel, final=final),
        name="peer_resid",
        grid=(t // tm,),
        in_specs=[
            pl.BlockSpec((tm, d), lambda i: (i, 0)),
            pl.BlockSpec((None, d, tm), lambda i: (i // per_p, 0, i % per_p)),
            pl.BlockSpec((None, 1, d), lambda i: (i // per_b, 0, 0)),
            pl.BlockSpec((1, d), lambda i: (0, 0)),
        ],
        out_specs=pl.BlockSpec((tm, d), lambda i: (i, 0)),
        out_shape=jax.ShapeDtypeStruct((t, d), F32),
        compiler_params=_cparams(("parallel",)),
    )(x1, yt, gt2, final_g)


def _layout_w_in(w_in_l):
    d = w_in_l.shape[0]
    o_a = 2 * GLA_QK + 2 * GLA_V
    o_u = o_a + GLA_RANK
    pad = jnp.zeros((d, PROJ_COLS - (w_in_l.shape[1] - GLA_RANK) - GLA_RANK), w_in_l.dtype)
    w = jnp.concatenate([w_in_l[:, :o_a], w_in_l[:, o_u:], w_in_l[:, o_a:o_u], pad], axis=1)
    w = w.astype(BF16).reshape(d, PROJ_COLS // PROJ_TN, PROJ_TN)
    return w.transpose(1, 0, 2)


def kernel(x, c, ada_w, ada_b, norm1_g, w_in, gla_w_a2, gla_b_a, gla_norm_g, gmlp_vnorm_g,
           gmlp_ws, gmlp_b, gmlp_out_g, w_out, norm2_g, peer_wq, peer_k1, peer_k2, peer_u,
           peer_v, final_g):
    batch, seq, d = x.shape
    depth = ada_w.shape[0]
    t = batch * seq
    xf = x.reshape(t, d)

    mod = _modulation(c, ada_w, ada_b)
    causal = jnp.tril(jnp.ones((GMLP_CHUNK, GMLP_CHUNK), F32))

    for l in range(depth):
        sh1, sc1, gt1, sh2, sc2, gt2 = [m.reshape(batch, 1, d) for m in jnp.split(mod[l], 6, axis=-1)]

        proj = _inproj(xf, norm1_g[l].reshape(1, d), sc1, sh1, _layout_w_in(w_in[l]), seq)

        wa2p = jnp.zeros((LANES, GLA_QK), F32).at[:GLA_RANK].set(gla_w_a2[l])
        y_gla = _gla(proj, wa2p, gla_b_a[l].reshape(1, GLA_QK),
                     gla_norm_g[l].reshape(1, GLA_V), batch, seq)

        bias_full = jnp.repeat(gmlp_b[l].T, GMLP_DH, axis=1)
        y_gmlp = _gmlp(proj, (gmlp_ws[l] * causal).astype(BF16), bias_full,
                       gmlp_vnorm_g[l].reshape(1, GMLP_WIDTH), gmlp_out_g[l].reshape(1, GMLP_WIDTH))

        x1, h2, h2t = _outproj(y_gla, y_gmlp, w_out[l].astype(BF16), xf, gt1,
                          norm2_g[l].reshape(1, d), sc2, sh2, seq)

        rank2, p2, n1, p1 = _route(h2, peer_wq[l].astype(BF16), peer_k1[l].astype(BF16),
                                   peer_k2[l].astype(BF16))
        pvt = peer_v[l].reshape(N_EXPERTS // PEER_TE, PEER_TE, d).transpose(0, 2, 1).astype(BF16)
        yt = _peer_dense(h2t, peer_u[l].astype(BF16), pvt, rank2, p2, n1, p1)
        xf = _resid(x1, yt, gt2, final_g.reshape(1, d), seq, final=(l == depth - 1))

    return xf.reshape(batch, seq, d)
```

```python
import functools

import jax
import jax.numpy as jnp
from jax import lax
from jax.experimental import pallas as pl
from jax.experimental.pallas import tpu as pltpu

F32 = jnp.float32
BF16 = jnp.bfloat16

D_MODEL = 2048
EPS = 1e-6
GLA_HEADS = 4
GLA_DV = 256
GLA_DK = 128
GLA_QK = GLA_HEADS * GLA_DK
GLA_V = GLA_HEADS * GLA_DV
GLA_RANK = 16
GLA_CHUNK = 64
GMLP_HEADS = 8
GMLP_WIDTH = 1024
GMLP_DH = 128
GMLP_CHUNK = 128
N_KEYS = 128
N_EXPERTS = N_KEYS * N_KEYS
PEER_HEADS = 8
PEER_TOPK = 16
PEER_DQ = 256

LANES = 128
SUBLANES = 8
PROJ_COLS = 5376
COL_U = 3
COL_VSP = 4
COL_A = 40
VMEM_LIMIT = 56 * 1024 * 1024

NEG_INF = float("-inf")


def _cparams(sem):
    return pltpu.CompilerParams(dimension_semantics=sem, vmem_limit_bytes=VMEM_LIMIT)


def _gelu(x):
    c = 0.7978845608028654
    return 0.5 * x * (1.0 + jnp.tanh(c * (x + 0.044715 * (x * x * x))))


def _split3(x):
    hi = x.astype(BF16)
    r1 = x - hi.astype(F32)
    mid = r1.astype(BF16)
    lo = (r1 - mid.astype(F32)).astype(BF16)
    return hi, mid, lo


def _dot(a, b):
    return jnp.dot(a, b, preferred_element_type=F32)


def _dot_nt(a, b):
    return lax.dot_general(a, b, (((1,), (1,)), ((), ())), preferred_element_type=F32)


def _dot_tn(a, b):
    return lax.dot_general(a, b, (((0,), (0,)), ((), ())), preferred_element_type=F32)


def _mod_kernel(c_ref, w_ref, b_ref, o_ref):
    c = c_ref[...]
    cond = c / (1.0 + jnp.exp(-c))
    acc = jnp.zeros(o_ref.shape, F32)
    w = w_ref[...]
    w_parts = _split3(w)
    c_parts = _split3(cond)
    for ci, wi in ((0, 0), (0, 1), (1, 0), (0, 2), (1, 1), (2, 0)):
        acc = acc + _dot(c_parts[ci], w_parts[wi])
    o_ref[...] = acc + b_ref[...]


def _modulation(c, ada_w, ada_b):
    depth, d, n = ada_w.shape
    b = c.shape[0]
    rows = 16
    cpad = jnp.zeros((rows, d), F32).at[:b].set(c)
    tn = 512
    out = pl.pallas_call(
        _mod_kernel,
        name="adaln_mod",
        grid=(depth, n // tn),
        in_specs=[
            pl.BlockSpec((rows, d), lambda l, j: (0, 0)),
            pl.BlockSpec((None, d, tn), lambda l, j: (l, 0, j)),
            pl.BlockSpec((None, 1, tn), lambda l, j: (l, 0, j)),
        ],
        out_specs=pl.BlockSpec((None, rows, tn), lambda l, j: (l, 0, j)),
        out_shape=jax.ShapeDtypeStruct((depth, rows, n), F32),
        compiler_params=_cparams(("parallel", "parallel")),
    )(cpad, ada_w, ada_b.reshape(depth, 1, n))
    return out[:, :b]


def _inproj_kernel(x_ref, g_ref, sc_ref, sh_ref, w_ref, o_ref, h_scr):
    @pl.when(pl.program_id(1) == 0)
    def _():
        x = x_ref[...]
        ms = jnp.mean(x * x, axis=-1, keepdims=True)
        y = x * lax.rsqrt(ms + EPS) * g_ref[...]
        h_scr[...] = (y * (1.0 + sc_ref[...]) + sh_ref[...]).astype(BF16)

    o_ref[...] = _dot(h_scr[...], w_ref[...])


PROJ_TN = 768


def _inproj(x, g, sc, sh, w, seq):
    t, d = x.shape
    n_tiles, _, tn = w.shape
    n = n_tiles * tn
    tm = min(1024, seq)
    per_b = seq // tm
    return pl.pallas_call(
        _inproj_kernel,
        name="norm_inproj",
        grid=(t // tm, n_tiles),
        in_specs=[
            pl.BlockSpec((tm, d), lambda i, j: (i, 0)),
            pl.BlockSpec((1, d), lambda i, j: (0, 0)),
            pl.BlockSpec((None, 1, d), lambda i, j: (i // per_b, 0, 0)),
            pl.BlockSpec((None, 1, d), lambda i, j: (i // per_b, 0, 0)),
            pl.BlockSpec((None, d, tn), lambda i, j: (j, 0, 0)),
        ],
        out_specs=pl.BlockSpec((tm, tn), lambda i, j: (i, j)),
        out_shape=jax.ShapeDtypeStruct((t, n), F32),
        scratch_shapes=[pltpu.VMEM((tm, d), BF16)],
        compiler_params=_cparams(("parallel", "arbitrary")),
    )(x, g, sc, sh, w)


GLA_LEVELS = (32, 16, 8, 4, 2, 1)


def _gla_kernel(q_ref, k_ref, v_ref, r_ref, a_ref, wa_ref, ba_ref, g_ref, o_ref,
                state_ref, *, n_chunks):
    C = GLA_CHUNK

    @pl.when(pl.program_id(1) == 0)
    def _():
        state_ref[...] = jnp.zeros(state_ref.shape, F32)

    nl = len(GLA_LEVELS) + 1
    ri = lax.broadcasted_iota(jnp.int32, (C, C), 0)
    ci = lax.broadcasted_iota(jnp.int32, (C, C), 1)
    pieces = [ci <= ri]
    masks = [ci == ri]
    for s in GLA_LEVELS:
        blk_r = ri // (2 * s)
        bound = blk_r * (2 * s) + (s - 1)
        pieces.append(ci <= bound)
        masks.append((blk_r == ci // (2 * s)) & (ri % (2 * s) >= s) & (ci % (2 * s) < s))
    prefix = jnp.concatenate([jnp.where(p, 1.0, 0.0).astype(BF16) for p in pieces], axis=0)

    wa = wa_ref[...].astype(BF16)
    ba = ba_ref[...]
    gain = g_ref[...]
    scale = GLA_DK ** -0.5

    def chunk(c, carry):
        off = pl.multiple_of(c * C, C)
        rows = pl.ds(off, C)
        for b in range(q_ref.shape[0]):
            z = _dot(a_ref[b, rows, :].astype(BF16), wa) + ba
            la = -(jnp.maximum(-z, 0.0) + jnp.log1p(jnp.exp(-jnp.abs(z)))) * (1.0 / 16.0)
            hi, mid, lo = _split3(la)
            pref_all = _dot(prefix, hi) + _dot(prefix, mid) + _dot(prefix, lo)

            for h in range(GLA_HEADS):
                kcols = slice(h * GLA_DK, (h + 1) * GLA_DK)
                vcols = slice(h * GLA_DV, (h + 1) * GLA_DV)
                pref = pref_all[:, kcols]
                q = q_ref[b, rows, kcols] * scale
                k = k_ref[b, rows, kcols]
                v = v_ref[b, rows, vcols].astype(BF16)
                G = pref[0:C]
                g_last = G[C - 1:C, :]

                attn = jnp.where(masks[0], _dot_nt(q.astype(BF16), k.astype(BF16)), 0.0)
                for l in range(1, nl):
                    e = jnp.exp(-jnp.abs(G - pref[l * C:(l + 1) * C]))
                    a_l = _dot_nt((q * e).astype(BF16), (k * e).astype(BF16))
                    attn = attn + jnp.where(masks[l], a_l, 0.0)

                st = state_ref[b, h]
                o = (_dot(attn.astype(BF16), v)
                     + _dot_nt((q * jnp.exp(G)).astype(BF16), st.astype(BF16)))
                k_dec = (k * jnp.exp(g_last - G)).astype(BF16)
                state_ref[b, h] = st * jnp.exp(g_last) + _dot_tn(v, k_dec)

                y = o * lax.rsqrt(jnp.mean(o * o, axis=-1, keepdims=True) + EPS) * gain[:, vcols]
                r = r_ref[b, rows, vcols]
                o_ref[b, rows, vcols] = (y * (r / (1.0 + jnp.exp(-r)))).astype(o_ref.dtype)
        return carry

    lax.fori_loop(0, n_chunks, chunk, 0, unroll=2)


def _gla(proj, wa2p, ba, norm_g, batch, seq):
    lc = min(512, seq)
    nb = 2 if batch % 2 == 0 else 1
    blk = lambda width, col: pl.BlockSpec((nb, lc, width), lambda b, s: (b, s, col))
    return pl.pallas_call(
        functools.partial(_gla_kernel, n_chunks=lc // GLA_CHUNK),
        name="gla",
        grid=(batch // nb, seq // lc),
        in_specs=[
            blk(GLA_QK, 0),
            blk(GLA_QK, 1),
            blk(GLA_V, 1),
            blk(GLA_V, 2),
            blk(LANES, COL_A),
            pl.BlockSpec((LANES, GLA_QK), lambda b, s: (0, 0)),
            pl.BlockSpec((1, GLA_QK), lambda b, s: (0, 0)),
            pl.BlockSpec((1, GLA_V), lambda b, s: (0, 0)),
        ],
        out_specs=blk(GLA_V, 0),
        out_shape=jax.ShapeDtypeStruct((batch, seq, GLA_V), BF16),
        scratch_shapes=[pltpu.VMEM((nb, GLA_HEADS, GLA_DV, GLA_DK), F32)],
        compiler_params=_cparams(("parallel", "arbitrary")),
    )(proj, proj, proj, proj, proj, wa2p, ba, norm_g)


def _gmlp_chunk(u_ref, v_ref, ws_ref, bias_ref, vg_ref, og_ref, ym_scr, rows):
    for h in range(GMLP_HEADS):
        cols = slice(h * GMLP_DH, (h + 1) * GMLP_DH)
        v = _gelu(v_ref[rows, cols])
        v = v * lax.rsqrt(jnp.mean(v * v, axis=-1, keepdims=True) + EPS) * vg_ref[:, cols]
        sv = _dot(ws_ref[h], v.astype(BF16)) + bias_ref[:, cols]
        y = _gelu(u_ref[rows, cols]) * sv
        y = y * lax.rsqrt(jnp.mean(y * y, axis=-1, keepdims=True) + EPS) * og_ref[:, cols]
        ym_scr[rows, cols] = y.astype(ym_scr.dtype)


def _outproj_kernel(yg_ref, u_ref, v_ref, ws_ref, bias_ref, vg_ref, og_ref, wg_ref, wm_ref,
                    x_ref, gt_ref, g2_ref, sc_ref, sh_ref, x1_ref, h2_ref, h2t_ref, ym_scr):
    mix = _dot(yg_ref[...], wg_ref[...])
    for c in range(yg_ref.shape[0] // GMLP_CHUNK):
        _gmlp_chunk(u_ref, v_ref, ws_ref, bias_ref, vg_ref, og_ref, ym_scr,
                    slice(c * GMLP_CHUNK, (c + 1) * GMLP_CHUNK))
    mix = mix + _dot(ym_scr[...], wm_ref[...])
    x1 = x_ref[...] + gt_ref[...] * mix
    x1_ref[...] = x1
    y = x1 * lax.rsqrt(jnp.mean(x1 * x1, axis=-1, keepdims=True) + EPS) * g2_ref[...]
    h2 = y * (1.0 + sc_ref[...]) + sh_ref[...]
    h2_ref[...] = h2.astype(BF16)
    h2t_ref[...] = h2.T.astype(BF16)


def _peer_tm(t):
    return min(512, t)


def _outproj(yg, proj, ws_causal, bias_full, vnorm_g, out_g, w_out, x, gt1, g2, sc2, sh2, seq):
    t, d = x.shape
    tm = min(256, seq)
    per_b = seq // tm
    half = d // 2
    ptm = _peer_tm(t)
    per_p = ptm // tm
    c = GMLP_CHUNK
    bvec = pl.BlockSpec((None, 1, d), lambda i: (i // per_b, 0, 0))
    return pl.pallas_call(
        _outproj_kernel,
        name="outproj",
        grid=(t // tm,),
        in_specs=[
            pl.BlockSpec((tm, half), lambda i: (i, 0)),
            pl.BlockSpec((tm, GMLP_WIDTH), lambda i: (i, COL_U)),
            pl.BlockSpec((tm, GMLP_WIDTH), lambda i: (i, COL_VSP)),
            pl.BlockSpec((GMLP_HEADS, c, c), lambda i: (0, 0, 0)),
            pl.BlockSpec((c, GMLP_WIDTH), lambda i: (0, 0)),
            pl.BlockSpec((1, GMLP_WIDTH), lambda i: (0, 0)),
            pl.BlockSpec((1, GMLP_WIDTH), lambda i: (0, 0)),
            pl.BlockSpec((half, d), lambda i: (0, 0)),
            pl.BlockSpec((half, d), lambda i: (1, 0)),
            pl.BlockSpec((tm, d), lambda i: (i, 0)),
            bvec,
            pl.BlockSpec((1, d), lambda i: (0, 0)),
            bvec,
            bvec,
        ],
        out_specs=[pl.BlockSpec((tm, d), lambda i: (i, 0)),
                   pl.BlockSpec((tm, d), lambda i: (i, 0)),
                   pl.BlockSpec((None, d, tm), lambda i: (i // per_p, 0, i % per_p))],
        out_shape=[jax.ShapeDtypeStruct((t, d), F32), jax.ShapeDtypeStruct((t, d), BF16),
                   jax.ShapeDtypeStruct((t // ptm, d, ptm), BF16)],
        scratch_shapes=[pltpu.VMEM((tm, GMLP_WIDTH), BF16)],
        compiler_params=_cparams(("parallel",)),
    )(yg, proj, proj, ws_causal, bias_full, vnorm_g, out_g, w_out, w_out, x, gt1, g2, sc2, sh2)


CAND_COLS = tuple(PEER_TOPK // (r + 1) for r in range(PEER_TOPK))
BIG_IDX = float(1 << 20)


def _top16(s, vals_ref, ties, want_rank):
    key = lax.broadcasted_iota(jnp.int32, s.shape, 0).astype(F32)
    rank = jnp.full(s.shape, float(PEER_TOPK), F32) if want_rank else None
    for r in range(PEER_TOPK):
        m = jnp.max(s, axis=0, keepdims=True)
        sel = s == m
        if ties:
            first = jnp.min(jnp.where(sel, key, BIG_IDX), axis=0, keepdims=True)
            sel = key == first
        if want_rank:
            rank = jnp.where(sel, float(r), rank)
        s = jnp.where(sel, NEG_INF, s)
        vals_ref[r:r + 1, :] = m
    marked = jnp.sum(jnp.where(s == NEG_INF, 1.0, 0.0), axis=0, keepdims=True)
    return rank, marked


def _route_tile(s1, s2, v1_scr, v2_scr, ties):
    width = s1.shape[1]
    sub = lax.broadcasted_iota(jnp.int32, (SUBLANES, width), 0).astype(F32)
    rank1, marked1 = _top16(s1, v1_scr, ties, want_rank=ties)
    rank2, marked2 = _top16(s2, v2_scr, ties, want_rank=True)

    v2a = v2_scr[0:8, :]
    v2b = v2_scr[8:16, :]
    cands = [v1_scr[0:1, :] + v2a, v1_scr[0:1, :] + v2b]
    ids = [sub, sub + 8]
    for r in range(1, 8):
        cands.append(jnp.where(sub < CAND_COLS[r], v1_scr[r:r + 1, :] + v2a, NEG_INF))
        ids.append(sub + PEER_TOPK * r)
    cands.append(v1_scr[8:16, :] + v2_scr[0:1, :])
    ids.append((sub + 8) * PEER_TOPK)
    taken = [jnp.zeros((SUBLANES, width), F32) for _ in cands]
    mx = v1_scr[0:1, :] + v2_scr[0:1, :]
    zsum = jnp.zeros((1, width), F32)
    for _ in range(PEER_TOPK):
        m = functools.reduce(jnp.maximum, cands)
        m = jnp.max(m, axis=0, keepdims=True)
        sels = [c == m for c in cands]
        if ties:
            first = functools.reduce(
                jnp.minimum, [jnp.where(s, i, BIG_IDX) for s, i in zip(sels, ids)])
            first = jnp.min(first, axis=0, keepdims=True)
            sels = [i == first for i in ids]
        cands = [jnp.where(s, NEG_INF, c) for s, c in zip(sels, cands)]
        taken = [jnp.where(s, 1.0, t) for s, t in zip(sels, taken)]
        zsum = zsum + jnp.exp(m - mx)

    counts = [jnp.sum(taken[0] + taken[1], axis=0, keepdims=True)]
    for r in range(1, 8):
        counts.append(jnp.sum(taken[r + 1], axis=0, keepdims=True))
    for r in range(8, PEER_TOPK):
        counts.append(taken[9][r - 8:r - 7, :])
    n1 = jnp.zeros(s1.shape, F32)
    for r in range(PEER_TOPK):
        hit = (rank1 == r) if ties else (s1 == v1_scr[r:r + 1, :])
        n1 = jnp.where(hit, counts[r], n1)

    p1 = jnp.exp(s1 - v1_scr[0:1, :]) * (1.0 / zsum)
    p2 = jnp.exp(s2 - v2_scr[0:1, :])
    return n1, p1, rank2, p2, [marked1, marked2, functools.reduce(jnp.add, counts)]


def _route_kernel(h_ref, wq_ref, k1_ref, k2_ref, rank2_ref, p2_ref, n1_ref, p1_ref,
                  s1_scr, s2_scr, v1_scr, v2_scr):
    half = PEER_DQ // 2

    def scores(h):
        q = _dot(h_ref[...], wq_ref[h])
        s1_scr[h] = _dot_nt(k1_ref[h], q[:, :half].astype(BF16))
        s2_scr[h] = _dot_nt(k2_ref[h], q[:, half:].astype(BF16))

    scores(0)

    def body(h, carry):
        s1 = s1_scr[h]
        s2 = s2_scr[h]
        scores(jnp.minimum(h + 1, PEER_HEADS - 1))

        def run(ties):
            n1, p1, rank2, p2, marked = _route_tile(s1, s2, v1_scr, v2_scr, ties)
            n1_ref[h] = n1
            p1_ref[h] = p1
            rank2_ref[h] = rank2.astype(rank2_ref.dtype)
            p2_ref[h] = p2.astype(p2_ref.dtype)
            return marked

        marked = run(ties=False)
        bad = functools.reduce(
            jnp.maximum, [jnp.where(mk == PEER_TOPK, 0.0, 1.0) for mk in marked])

        @pl.when(jnp.max(bad) > 0.0)
        def _():
            run(ties=True)

        return carry

    lax.fori_loop(0, PEER_HEADS, body, 0)


def _route(h2, wq, k1, k2):
    t, d = h2.shape
    tm = _peer_tm(t)
    hk = (PEER_HEADS, N_KEYS, tm)
    tiled = (t // tm,) + hk
    out_spec = pl.BlockSpec((None,) + hk, lambda i: (i, 0, 0, 0))
    kspec = pl.BlockSpec((PEER_HEADS, N_KEYS, PEER_DQ // 2), lambda i: (0, 0, 0))
    wq_heads = wq.reshape(d, PEER_HEADS, PEER_DQ).transpose(1, 0, 2)
    return pl.pallas_call(
        _route_kernel,
        name="peer_route",
        grid=(t // tm,),
        in_specs=[
            pl.BlockSpec((tm, d), lambda i: (i, 0)),
            pl.BlockSpec(wq_heads.shape, lambda i: (0, 0, 0)),
            kspec, kspec,
        ],
        out_specs=[out_spec, out_spec, out_spec, out_spec],
        out_shape=[
            jax.ShapeDtypeStruct(tiled, BF16),
            jax.ShapeDtypeStruct(tiled, BF16),
            jax.ShapeDtypeStruct(tiled, F32),
            jax.ShapeDtypeStruct(tiled, F32),
        ],
        scratch_shapes=[
            pltpu.VMEM(hk, F32), pltpu.VMEM(hk, F32),
            pltpu.VMEM((PEER_TOPK, tm), F32), pltpu.VMEM((PEER_TOPK, tm), F32),
        ],
        compiler_params=_cparams(("parallel",)),
    )(h2, wq_heads, k1, k2)


def _peer_kernel(ht_ref, pu_ref, pvt_ref, rank2_ref, p2_ref, n1_ref, p1_ref, o_ref,
                 pa_scr, pb_scr, *, a_rows, n_e):
    j = pl.program_id(1)
    group = 1
    n_groups = a_rows // group
    o_rows = o_ref.shape[0] // n_groups

    def activations(k, p_scr):
        base = pl.multiple_of(k * (group * N_KEYS), group * N_KEYS)
        scores = _dot(pu_ref[pl.ds(base, group * N_KEYS), :], ht_ref[...])
        for g in range(group):
            a = j * a_rows + k * group + g
            w = None
            for h in range(PEER_HEADS):
                n1 = n1_ref[h, pl.ds(a, 1), :].astype(BF16)
                p1 = p1_ref[h, pl.ds(a, 1), :].astype(BF16)
                term = jnp.where(rank2_ref[h] < n1, p2_ref[h], jnp.zeros((), BF16)) * p1
                w = term if w is None else w + term
            act = _gelu(scores[g * N_KEYS:(g + 1) * N_KEYS, :])
            p_scr[pl.ds(base + g * N_KEYS, N_KEYS), :] = (act * w.astype(F32)).astype(BF16)

    def accumulate(k, p_scr):
        rows = pl.ds(pl.multiple_of(k * o_rows, o_rows), o_rows)
        o_ref[rows, :] += _dot(pvt_ref[rows, :], p_scr[...])

    def both(p_new, p_old):
        def body(k, carry):
            activations(k, p_new)
            accumulate(k, p_old)
            return carry
        lax.fori_loop(0, n_groups, body, 0, unroll=2)

    @pl.when(j == 0)
    def _():
        o_ref[...] = jnp.zeros(o_ref.shape, F32)
        lax.fori_loop(0, n_groups, lambda k, c: (activations(k, pa_scr), c)[1], 0)

    @pl.when((j > 0) & (j < n_e) & (j % 2 == 0))
    def _():
        both(pa_scr, pb_scr)

    @pl.when((j < n_e) & (j % 2 == 1))
    def _():
        both(pb_scr, pa_scr)

    @pl.when(j == n_e)
    def _():
        p_last = pb_scr if n_e % 2 == 0 else pa_scr
        lax.fori_loop(0, n_groups, lambda k, c: (accumulate(k, p_last), c)[1], 0)


PEER_TE = 512


def _peer_dense(ht, pu, pvt, rank2, p2, n1, p1):
    n_t, d, tm = ht.shape
    te = PEER_TE
    n_e = N_EXPERTS // te
    rspec = pl.BlockSpec((None, PEER_HEADS, N_KEYS, tm), lambda i, j: (i, 0, 0, 0))
    return pl.pallas_call(
        functools.partial(_peer_kernel, a_rows=te // N_KEYS, n_e=n_e),
        name="peer_dense",
        grid=(n_t, n_e + 1),
        in_specs=[
            pl.BlockSpec((None, d, tm), lambda i, j: (i, 0, 0)),
            pl.BlockSpec((te, d), lambda i, j: (jnp.minimum(j, n_e - 1), 0)),
            pl.BlockSpec((None, d, te), lambda i, j: (jnp.maximum(j - 1, 0), 0, 0)),
            rspec, rspec, rspec, rspec,
        ],
        out_specs=pl.BlockSpec((None, d, tm), lambda i, j: (i, 0, 0)),
        out_shape=jax.ShapeDtypeStruct((n_t, d, tm), F32),
        scratch_shapes=[pltpu.VMEM((te, tm), BF16), pltpu.VMEM((te, tm), BF16)],
        compiler_params=_cparams(("parallel", "arbitrary")),
    )(ht, pu, pvt, rank2, p2, n1, p1)


def _resid_kernel(x_ref, yt_ref, gt_ref, g_ref, o_ref, *, final):
    x2 = x_ref[...] + gt_ref[...] * yt_ref[...].T
    if final:
        x2 = x2 * lax.rsqrt(jnp.mean(x2 * x2, axis=-1, keepdims=True) + EPS) * g_ref[...]
    o_ref[...] = x2


def _resid(x1, yt, gt2, final_g, seq, final):
    t, d = x1.shape
    tm = min(256, seq)
    per_b = seq // tm
    per_p = yt.shape[2] // tm
    return pl.pallas_call(
        functools.partial(_resid_kernel, final=final),
        name="peer_resid",
        grid=(t // tm,),
        in_specs=[
            pl.BlockSpec((tm, d), lambda i: (i, 0)),
            pl.BlockSpec((None, d, tm), lambda i: (i // per_p, 0, i % per_p)),
            pl.BlockSpec((None, 1, d), lambda i: (i // per_b, 0, 0)),
            pl.BlockSpec((1, d), lambda i: (0, 0)),
        ],
        out_specs=pl.BlockSpec((tm, d), lambda i: (i, 0)),
        out_shape=jax.ShapeDtypeStruct((t, d), F32),
        compiler_params=_cparams(("parallel",)),
    )(x1, yt, gt2, final_g)


def _layout_w_in(w_in_l):
    d = w_in_l.shape[0]
    o_a = 2 * GLA_QK + 2 * GLA_V
    o_u = o_a + GLA_RANK
    pad = jnp.zeros((d, PROJ_COLS - (w_in_l.shape[1] - GLA_RANK) - GLA_RANK), w_in_l.dtype)
    w = jnp.concatenate([w_in_l[:, :o_a], w_in_l[:, o_u:], w_in_l[:, o_a:o_u], pad], axis=1)
    w = w.astype(BF16).reshape(d, PROJ_COLS // PROJ_TN, PROJ_TN)
    return w.transpose(1, 0, 2)


def kernel(x, c, ada_w, ada_b, norm1_g, w_in, gla_w_a2, gla_b_a, gla_norm_g, gmlp_vnorm_g,
           gmlp_ws, gmlp_b, gmlp_out_g, w_out, norm2_g, peer_wq, peer_k1, peer_k2, peer_u,
           peer_v, final_g):
    batch, seq, d = x.shape
    depth = ada_w.shape[0]
    t = batch * seq
    xf = x.reshape(t, d)

    mod = _modulation(c, ada_w, ada_b)
    causal = jnp.tril(jnp.ones((GMLP_CHUNK, GMLP_CHUNK), F32))

    for l in range(depth):
        sh1, sc1, gt1, sh2, sc2, gt2 = [m.reshape(batch, 1, d) for m in jnp.split(mod[l], 6, axis=-1)]

        proj = _inproj(xf, norm1_g[l].reshape(1, d), sc1, sh1, _layout_w_in(w_in[l]), seq)

        wa2p = jnp.zeros((LANES, GLA_QK), F32).at[:GLA_RANK].set(gla_w_a2[l])
        y_gla = _gla(proj.reshape(batch, seq, PROJ_COLS), wa2p, gla_b_a[l].reshape(1, GLA_QK),
                     gla_norm_g[l].reshape(1, GLA_V), batch, seq).reshape(t, GLA_V)

        bias_full = jnp.repeat(gmlp_b[l].T, GMLP_DH, axis=1)
        x1, h2, h2t = _outproj(
            y_gla, proj, (gmlp_ws[l] * causal).astype(BF16), bias_full,
            gmlp_vnorm_g[l].reshape(1, GMLP_WIDTH), gmlp_out_g[l].reshape(1, GMLP_WIDTH),
            w_out[l].astype(BF16), xf, gt1, norm2_g[l].reshape(1, d), sc2, sh2, seq)

        rank2, p2, n1, p1 = _route(h2, peer_wq[l].astype(BF16), peer_k1[l].astype(BF16),
                                   peer_k2[l].astype(BF16))
        pvt = peer_v[l].reshape(N_EXPERTS // PEER_TE, PEER_TE, d).transpose(0, 2, 1).astype(BF16)
        yt = _peer_dense(h2t, peer_u[l].astype(BF16), pvt, rank2, p2, n1, p1)
        xf = _resid(x1, yt, gt2, final_g.reshape(1, d), seq, final=(l == depth - 1))

    return xf.reshape(batch, seq, d)
```

```python
import functools

import jax
import jax.numpy as jnp
from jax import lax
from jax.experimental import pallas as pl
from jax.experimental.pallas import tpu as pltpu

F32 = jnp.float32
BF16 = jnp.bfloat16

D_MODEL = 2048
EPS = 1e-6
GLA_HEADS = 4
GLA_DV = 256
GLA_DK = 128
GLA_QK = GLA_HEADS * GLA_DK
GLA_V = GLA_HEADS * GLA_DV
GLA_RANK = 16
GLA_CHUNK = 64
GMLP_HEADS = 8
GMLP_WIDTH = 1024
GMLP_DH = 128
GMLP_CHUNK = 128
N_KEYS = 128
N_EXPERTS = N_KEYS * N_KEYS
PEER_HEADS = 8
PEER_TOPK = 16
PEER_DQ = 256

LANES = 128
SUBLANES = 8
PROJ_COLS = 5376
COL_U = 3
COL_VSP = 4
COL_A = 40
VMEM_LIMIT = 56 * 1024 * 1024

NEG_INF = float("-inf")


def _cparams(sem):
    return pltpu.CompilerParams(dimension_semantics=sem, vmem_limit_bytes=VMEM_LIMIT)


def _gelu(x):
    c = 0.7978845608028654
    return 0.5 * x * (1.0 + jnp.tanh(c * (x + 0.044715 * (x * x * x))))


def _split3(x):
    hi = x.astype(BF16)
    r1 = x - hi.astype(F32)
    mid = r1.astype(BF16)
    lo = (r1 - mid.astype(F32)).astype(BF16)
    return hi, mid, lo


def _dot(a, b):
    return jnp.dot(a, b, preferred_element_type=F32)


def _dot_nt(a, b):
    return lax.dot_general(a, b, (((1,), (1,)), ((), ())), preferred_element_type=F32)


def _dot_tn(a, b):
    return lax.dot_general(a, b, (((0,), (0,)), ((), ())), preferred_element_type=F32)


def _mod_kernel(c_ref, w_ref, b_ref, o_ref):
    c = c_ref[...]
    cond = c / (1.0 + jnp.exp(-c))
    acc = jnp.zeros(o_ref.shape, F32)
    w = w_ref[...]
    w_parts = _split3(w)
    c_parts = _split3(cond)
    for ci, wi in ((0, 0), (0, 1), (1, 0), (0, 2), (1, 1), (2, 0)):
        acc = acc + _dot(c_parts[ci], w_parts[wi])
    o_ref[...] = acc + b_ref[...]


def _modulation(c, ada_w, ada_b):
    depth, d, n = ada_w.shape
    b = c.shape[0]
    rows = 16
    cpad = jnp.zeros((rows, d), F32).at[:b].set(c)
    tn = 512
    out = pl.pallas_call(
        _mod_kernel,
        name="adaln_mod",
        grid=(depth, n // tn),
        in_specs=[
            pl.BlockSpec((rows, d), lambda l, j: (0, 0)),
            pl.BlockSpec((None, d, tn), lambda l, j: (l, 0, j)),
            pl.BlockSpec((None, 1, tn), lambda l, j: (l, 0, j)),
        ],
        out_specs=pl.BlockSpec((None, rows, tn), lambda l, j: (l, 0, j)),
        out_shape=jax.ShapeDtypeStruct((depth, rows, n), F32),
        compiler_params=_cparams(("parallel", "parallel")),
    )(cpad, ada_w, ada_b.reshape(depth, 1, n))
    return out[:, :b]


def _inproj_kernel(x_ref, g_ref, sc_ref, sh_ref, w_ref, o_ref, h_scr):
    @pl.when(pl.program_id(1) == 0)
    def _():
        x = x_ref[...]
        ms = jnp.mean(x * x, axis=-1, keepdims=True)
        y = x * lax.rsqrt(ms + EPS) * g_ref[...]
        h_scr[...] = (y * (1.0 + sc_ref[...]) + sh_ref[...]).astype(BF16)

    o_ref[...] = _dot(h_scr[...], w_ref[...])


PROJ_TN = 768


def _inproj(x, g, sc, sh, w, seq):
    t, d = x.shape
    n_tiles, _, tn = w.shape
    n = n_tiles * tn
    tm = min(1024, seq)
    per_b = seq // tm
    return pl.pallas_call(
        _inproj_kernel,
        name="norm_inproj",
        grid=(t // tm, n_tiles),
        in_specs=[
            pl.BlockSpec((tm, d), lambda i, j: (i, 0)),
            pl.BlockSpec((1, d), lambda i, j: (0, 0)),
            pl.BlockSpec((None, 1, d), lambda i, j: (i // per_b, 0, 0)),
            pl.BlockSpec((None, 1, d), lambda i, j: (i // per_b, 0, 0)),
            pl.BlockSpec((None, d, tn), lambda i, j: (j, 0, 0)),
        ],
        out_specs=pl.BlockSpec((tm, tn), lambda i, j: (i, j)),
        out_shape=jax.ShapeDtypeStruct((t, n), F32),
        scratch_shapes=[pltpu.VMEM((tm, d), BF16)],
        compiler_params=_cparams(("parallel", "arbitrary")),
    )(x, g, sc, sh, w)


GLA_LEVELS = (32, 16, 8, 4, 2, 1)


def _gla_kernel(q_ref, k_ref, v_ref, r_ref, a_ref, wa_ref, ba_ref, g_ref, o_ref,
                state_ref, *, n_chunks):
    C = GLA_CHUNK

    @pl.when(pl.program_id(1) == 0)
    def _():
        state_ref[...] = jnp.zeros(state_ref.shape, F32)

    nl = len(GLA_LEVELS) + 1
    ri = lax.broadcasted_iota(jnp.int32, (C, C), 0)
    ci = lax.broadcasted_iota(jnp.int32, (C, C), 1)
    pieces = [ci <= ri]
    masks = [ci == ri]
    for s in GLA_LEVELS:
        blk_r = ri // (2 * s)
        bound = blk_r * (2 * s) + (s - 1)
        pieces.append(ci <= bound)
        masks.append((blk_r == ci // (2 * s)) & (ri % (2 * s) >= s) & (ci % (2 * s) < s))
    prefix = jnp.concatenate([jnp.where(p, 1.0, 0.0).astype(BF16) for p in pieces], axis=0)

    wa = wa_ref[...].astype(BF16)
    ba = ba_ref[...]
    gain = g_ref[...]
    scale = GLA_DK ** -0.5

    def chunk(c, carry):
        off = pl.multiple_of(c * C, C)
        rows = pl.ds(off, C)
        for b in range(q_ref.shape[0]):
            z = _dot(a_ref[b, rows, :].astype(BF16), wa) + ba
            la = -(jnp.maximum(-z, 0.0) + jnp.log1p(jnp.exp(-jnp.abs(z)))) * (1.0 / 16.0)
            hi, mid, lo = _split3(la)
            pref_all = _dot(prefix, hi) + _dot(prefix, mid) + _dot(prefix, lo)

            for h in range(GLA_HEADS):
                kcols = slice(h * GLA_DK, (h + 1) * GLA_DK)
                vcols = slice(h * GLA_DV, (h + 1) * GLA_DV)
                pref = pref_all[:, kcols]
                q = q_ref[b, rows, kcols] * scale
                k = k_ref[b, rows, kcols]
                v = v_ref[b, rows, vcols].astype(BF16)
                G = pref[0:C]
                g_last = G[C - 1:C, :]

                attn = jnp.where(masks[0], _dot_nt(q.astype(BF16), k.astype(BF16)), 0.0)
                for l in range(1, nl):
                    e = jnp.exp(-jnp.abs(G - pref[l * C:(l + 1) * C]))
                    a_l = _dot_nt((q * e).astype(BF16), (k * e).astype(BF16))
                    attn = attn + jnp.where(masks[l], a_l, 0.0)

                st = state_ref[b, h]
                o = (_dot(attn.astype(BF16), v)
                     + _dot_nt((q * jnp.exp(G)).astype(BF16), st.astype(BF16)))
                k_dec = (k * jnp.exp(g_last - G)).astype(BF16)
                state_ref[b, h] = st * jnp.exp(g_last) + _dot_tn(v, k_dec)

                y = o * lax.rsqrt(jnp.mean(o * o, axis=-1, keepdims=True) + EPS) * gain[:, vcols]
                r = r_ref[b, rows, vcols]
                o_ref[b, rows, vcols] = (y * (r / (1.0 + jnp.exp(-r)))).astype(o_ref.dtype)
        return carry

    lax.fori_loop(0, n_chunks, chunk, 0, unroll=2)


def _gla(proj, wa2p, ba, norm_g, batch, seq):
    lc = min(512, seq)
    nb = 2 if batch % 2 == 0 else 1
    blk = lambda width, col: pl.BlockSpec((nb, lc, width), lambda b, s: (b, s, col))
    return pl.pallas_call(
        functools.partial(_gla_kernel, n_chunks=lc // GLA_CHUNK),
        name="gla",
        grid=(batch // nb, seq // lc),
        in_specs=[
            blk(GLA_QK, 0),
            blk(GLA_QK, 1),
            blk(GLA_V, 1),
            blk(GLA_V, 2),
            blk(LANES, COL_A),
            pl.BlockSpec((LANES, GLA_QK), lambda b, s: (0, 0)),
            pl.BlockSpec((1, GLA_QK), lambda b, s: (0, 0)),
            pl.BlockSpec((1, GLA_V), lambda b, s: (0, 0)),
        ],
        out_specs=blk(GLA_V, 0),
        out_shape=jax.ShapeDtypeStruct((batch, seq, GLA_V), BF16),
        scratch_shapes=[pltpu.VMEM((nb, GLA_HEADS, GLA_DV, GLA_DK), F32)],
        compiler_params=_cparams(("parallel", "arbitrary")),
    )(proj, proj, proj, proj, proj, wa2p, ba, norm_g)


def _gmlp_chunk(u_ref, v_ref, ws_ref, bias_ref, vg_ref, og_ref, ym_scr, rows):
    for h in range(GMLP_HEADS):
        cols = slice(h * GMLP_DH, (h + 1) * GMLP_DH)
        v = _gelu(v_ref[rows, cols])
        v = v * lax.rsqrt(jnp.mean(v * v, axis=-1, keepdims=True) + EPS) * vg_ref[:, cols]
        sv = _dot(ws_ref[h], v.astype(BF16)) + bias_ref[:, cols]
        y = _gelu(u_ref[rows, cols]) * sv
        y = y * lax.rsqrt(jnp.mean(y * y, axis=-1, keepdims=True) + EPS) * og_ref[:, cols]
        ym_scr[rows, cols] = y.astype(ym_scr.dtype)


def _outproj_kernel(yg_ref, u_ref, v_ref, ws_ref, bias_ref, vg_ref, og_ref, wg_ref, wm_ref,
                    x_ref, gt_ref, g2_ref, sc_ref, sh_ref, x1_ref, h2_ref, h2t_ref, ym_scr):
    mix = _dot(yg_ref[...], wg_ref[...])
    for c in range(yg_ref.shape[0] // GMLP_CHUNK):
        _gmlp_chunk(u_ref, v_ref, ws_ref, bias_ref, vg_ref, og_ref, ym_scr,
                    slice(c * GMLP_CHUNK, (c + 1) * GMLP_CHUNK))
    mix = mix + _dot(ym_scr[...], wm_ref[...])
    x1 = x_ref[...] + gt_ref[...] * mix
    x1_ref[...] = x1
    y = x1 * lax.rsqrt(jnp.mean(x1 * x1, axis=-1, keepdims=True) + EPS) * g2_ref[...]
    h2 = y * (1.0 + sc_ref[...]) + sh_ref[...]
    h2_ref[...] = h2.astype(BF16)
    h2t_ref[...] = h2.T.astype(BF16)


def _peer_tm(t):
    return min(512, t)


def _outproj(yg, proj, ws_causal, bias_full, vnorm_g, out_g, w_out, x, gt1, g2, sc2, sh2, seq):
    t, d = x.shape
    tm = min(256, seq)
    per_b = seq // tm
    half = d // 2
    ptm = _peer_tm(t)
    per_p = ptm // tm
    c = GMLP_CHUNK
    bvec = pl.BlockSpec((None, 1, d), lambda i: (i // per_b, 0, 0))
    return pl.pallas_call(
        _outproj_kernel,
        name="outproj",
        grid=(t // tm,),
        in_specs=[
            pl.BlockSpec((tm, half), lambda i: (i, 0)),
            pl.BlockSpec((tm, GMLP_WIDTH), lambda i: (i, COL_U)),
            pl.BlockSpec((tm, GMLP_WIDTH), lambda i: (i, COL_VSP)),
            pl.BlockSpec((GMLP_HEADS, c, c), lambda i: (0, 0, 0)),
            pl.BlockSpec((c, GMLP_WIDTH), lambda i: (0, 0)),
            pl.BlockSpec((1, GMLP_WIDTH), lambda i: (0, 0)),
            pl.BlockSpec((1, GMLP_WIDTH), lambda i: (0, 0)),
            pl.BlockSpec((half, d), lambda i: (0, 0)),
            pl.BlockSpec((half, d), lambda i: (1, 0)),
            pl.BlockSpec((tm, d), lambda i: (i, 0)),
            bvec,
            pl.BlockSpec((1, d), lambda i: (0, 0)),
            bvec,
            bvec,
        ],
        out_specs=[pl.BlockSpec((tm, d), lambda i: (i, 0)),
                   pl.BlockSpec((tm, d), lambda i: (i, 0)),
                   pl.BlockSpec((None, d, tm), lambda i: (i // per_p, 0, i % per_p))],
        out_shape=[jax.ShapeDtypeStruct((t, d), F32), jax.ShapeDtypeStruct((t, d), BF16),
                   jax.ShapeDtypeStruct((t // ptm, d, ptm), BF16)],
        scratch_shapes=[pltpu.VMEM((tm, GMLP_WIDTH), BF16)],
        compiler_params=_cparams(("parallel",)),
    )(yg, proj, proj, ws_causal, bias_full, vnorm_g, out_g, w_out, w_out, x, gt1, g2, sc2, sh2)


CAND_COLS = tuple(PEER_TOPK // (r + 1) for r in range(PEER_TOPK))
BIG_IDX = float(1 << 20)


def _top16(s, vals_ref, ties, want_rank):
    key = lax.broadcasted_iota(jnp.int32, s.shape, 0).astype(F32)
    rank = jnp.full(s.shape, float(PEER_TOPK), F32) if want_rank else None
    for r in range(PEER_TOPK):
        m = jnp.max(s, axis=0, keepdims=True)
        sel = s == m
        if ties:
            first = jnp.min(jnp.where(sel, key, BIG_IDX), axis=0, keepdims=True)
            sel = key == first
        if want_rank:
            rank = jnp.where(sel, float(r), rank)
        s = jnp.where(sel, NEG_INF, s)
        vals_ref[r:r + 1, :] = m
    marked = jnp.sum(jnp.where(s == NEG_INF, 1.0, 0.0), axis=0, keepdims=True)
    return rank, marked


def _route_tile(s1, s2, v1_scr, v2_scr, ties):
    width = s1.shape[1]
    sub = lax.broadcasted_iota(jnp.int32, (SUBLANES, width), 0).astype(F32)
    rank1, marked1 = _top16(s1, v1_scr, ties, want_rank=ties)
    rank2, marked2 = _top16(s2, v2_scr, ties, want_rank=True)

    v2a = v2_scr[0:8, :]
    v2b = v2_scr[8:16, :]
    cands = [v1_scr[0:1, :] + v2a, v1_scr[0:1, :] + v2b]
    ids = [sub, sub + 8]
    for r in range(1, 8):
        cands.append(jnp.where(sub < CAND_COLS[r], v1_scr[r:r + 1, :] + v2a, NEG_INF))
        ids.append(sub + PEER_TOPK * r)
    cands.append(v1_scr[8:16, :] + v2_scr[0:1, :])
    ids.append((sub + 8) * PEER_TOPK)
    taken = [jnp.zeros((SUBLANES, width), F32) for _ in cands]
    mx = v1_scr[0:1, :] + v2_scr[0:1, :]
    zsum = jnp.zeros((1, width), F32)
    for _ in range(PEER_TOPK):
        m = functools.reduce(jnp.maximum, cands)
        m = jnp.max(m, axis=0, keepdims=True)
        sels = [c == m for c in cands]
        if ties:
            first = functools.reduce(
                jnp.minimum, [jnp.where(s, i, BIG_IDX) for s, i in zip(sels, ids)])
            first = jnp.min(first, axis=0, keepdims=True)
            sels = [i == first for i in ids]
        cands = [jnp.where(s, NEG_INF, c) for s, c in zip(sels, cands)]
        taken = [jnp.where(s, 1.0, t) for s, t in zip(sels, taken)]
        zsum = zsum + jnp.exp(m - mx)

    counts = [jnp.sum(taken[0] + taken[1], axis=0, keepdims=True)]
    for r in range(1, 8):
        counts.append(jnp.sum(taken[r + 1], axis=0, keepdims=True))
    for r in range(8, PEER_TOPK):
        counts.append(taken[9][r - 8:r - 7, :])
    n1 = jnp.zeros(s1.shape, F32)
    for r in range(PEER_TOPK):
        hit = (rank1 == r) if ties else (s1 == v1_scr[r:r + 1, :])
        n1 = jnp.where(hit, counts[r], n1)

    p1 = jnp.exp(s1 - v1_scr[0:1, :]) * (1.0 / zsum)
    p2 = jnp.exp(s2 - v2_scr[0:1, :])
    return n1, p1, rank2, p2, [marked1, marked2, functools.reduce(jnp.add, counts)]


def _route_kernel(h_ref, wq_ref, k1_ref, k2_ref, rank2_ref, p2_ref, n1_ref, p1_ref,
                  s1_scr, s2_scr, v1_scr, v2_scr):
    half = PEER_DQ // 2

    def scores(h):
        q = _dot(h_ref[...], wq_ref[h])
        s1_scr[h] = _dot_nt(k1_ref[h], q[:, :half].astype(BF16))
        s2_scr[h] = _dot_nt(k2_ref[h], q[:, half:].astype(BF16))

    scores(0)

    def body(h, carry):
        s1 = s1_scr[h]
        s2 = s2_scr[h]
        scores(jnp.minimum(h + 1, PEER_HEADS - 1))

        def run(ties):
            n1, p1, rank2, p2, marked = _route_tile(s1, s2, v1_scr, v2_scr, ties)
            n1_ref[h] = n1
            p1_ref[h] = p1
            rank2_ref[h] = rank2.astype(rank2_ref.dtype)
            p2_ref[h] = p2.astype(p2_ref.dtype)
            return marked

        marked = run(ties=False)
        bad = functools.reduce(
            jnp.maximum, [jnp.where(mk == PEER_TOPK, 0.0, 1.0) for mk in marked])

        @pl.when(jnp.max(bad) > 0.0)
        def _():
            run(ties=True)

        return carry

    lax.fori_loop(0, PEER_HEADS, body, 0)


def _route(h2, wq, k1, k2):
    t, d = h2.shape
    tm = _peer_tm(t)
    hk = (PEER_HEADS, N_KEYS, tm)
    tiled = (t // tm,) + hk
    out_spec = pl.BlockSpec((None,) + hk, lambda i: (i, 0, 0, 0))
    kspec = pl.BlockSpec((PEER_HEADS, N_KEYS, PEER_DQ // 2), lambda i: (0, 0, 0))
    wq_heads = wq.reshape(d, PEER_HEADS, PEER_DQ).transpose(1, 0, 2)
    return pl.pallas_call(
        _route_kernel,
        name="peer_route",
        grid=(t // tm,),
        in_specs=[
            pl.BlockSpec((tm, d), lambda i: (i, 0)),
            pl.BlockSpec(wq_heads.shape, lambda i: (0, 0, 0)),
            kspec, kspec,
        ],
        out_specs=[out_spec, out_spec, out_spec, out_spec],
        out_shape=[
            jax.ShapeDtypeStruct(tiled, BF16),
            jax.ShapeDtypeStruct(tiled, BF16),
            jax.ShapeDtypeStruct(tiled, F32),
            jax.ShapeDtypeStruct(tiled, F32),
        ],
        scratch_shapes=[
            pltpu.VMEM(hk, F32), pltpu.VMEM(hk, F32),
            pltpu.VMEM((PEER_TOPK, tm), F32), pltpu.VMEM((PEER_TOPK, tm), F32),
        ],
        compiler_params=_cparams(("parallel",)),
    )(h2, wq_heads, k1, k2)


def _peer_kernel(ht_ref, pu_ref, pvt_ref, rank2_ref, p2_ref, n1_ref, p1_ref, o_ref,
                 pa_scr, pb_scr, *, a_rows, n_e):
    j = pl.program_id(1)
    group = 1
    acc_split = 1
    n_groups = a_rows // group
    o_rows = o_ref.shape[0] // n_groups

    def activations(k, p_scr):
        base = pl.multiple_of(k * (group * N_KEYS), group * N_KEYS)
        scores = _dot(pu_ref[pl.ds(base, group * N_KEYS), :], ht_ref[...])
        for g in range(group):
            a = j * a_rows + k * group + g
            w = None
            for h in range(PEER_HEADS):
                n1 = n1_ref[h, pl.ds(a, 1), :].astype(BF16)
                p1 = p1_ref[h, pl.ds(a, 1), :].astype(BF16)
                term = jnp.where(rank2_ref[h] < n1, p2_ref[h], jnp.zeros((), BF16)) * p1
                w = term if w is None else w + term
            act = _gelu(scores[g * N_KEYS:(g + 1) * N_KEYS, :])
            p_scr[pl.ds(base + g * N_KEYS, N_KEYS), :] = (act * w.astype(F32)).astype(BF16)

    def accumulate(k, p_scr):
        piece = o_rows // acc_split
        for s in range(acc_split):
            rows = pl.ds(pl.multiple_of(k * o_rows + s * piece, piece), piece)
            o_ref[rows, :] += _dot(pvt_ref[rows, :], p_scr[...])

    def both(p_new, p_old):
        def body(k, carry):
            activations(k, p_new)
            accumulate(k, p_old)
            return carry
        lax.fori_loop(0, n_groups, body, 0, unroll=n_groups)

    @pl.when(j == 0)
    def _():
        o_ref[...] = jnp.zeros(o_ref.shape, F32)
        lax.fori_loop(0, n_groups, lambda k, c: (activations(k, pa_scr), c)[1], 0)

    @pl.when((j > 0) & (j < n_e) & (j % 2 == 0))
    def _():
        both(pa_scr, pb_scr)

    @pl.when((j < n_e) & (j % 2 == 1))
    def _():
        both(pb_scr, pa_scr)

    @pl.when(j == n_e)
    def _():
        p_last = pb_scr if n_e % 2 == 0 else pa_scr
        lax.fori_loop(0, n_groups, lambda k, c: (accumulate(k, p_last), c)[1], 0)


PEER_TE = 512


def _peer_dense(ht, pu, pvt, rank2, p2, n1, p1):
    n_t, d, tm = ht.shape
    te = PEER_TE
    n_e = N_EXPERTS // te
    rspec = pl.BlockSpec((None, PEER_HEADS, N_KEYS, tm), lambda i, j: (i, 0, 0, 0))
    return pl.pallas_call(
        functools.partial(_peer_kernel, a_rows=te // N_KEYS, n_e=n_e),
        name="peer_dense",
        grid=(n_t, n_e + 1),
        in_specs=[
            pl.BlockSpec((None, d, tm), lambda i, j: (i, 0, 0)),
            pl.BlockSpec((te, d), lambda i, j: (jnp.minimum(j, n_e - 1), 0)),
            pl.BlockSpec((None, d, te), lambda i, j: (jnp.maximum(j - 1, 0), 0, 0)),
            rspec, rspec, rspec, rspec,
        ],
        out_specs=pl.BlockSpec((None, d, tm), lambda i, j: (i, 0, 0)),
        out_shape=jax.ShapeDtypeStruct((n_t, d, tm), F32),
        scratch_shapes=[pltpu.VMEM((te, tm), BF16), pltpu.VMEM((te, tm), BF16)],
        compiler_params=_cparams(("parallel", "arbitrary")),
    )(ht, pu, pvt, rank2, p2, n1, p1)


def _resid_kernel(x_ref, yt_ref, gt_ref, g_ref, o_ref, *, final):
    x2 = x_ref[...] + gt_ref[...] * yt_ref[...].T
    if final:
        x2 = x2 * lax.rsqrt(jnp.mean(x2 * x2, axis=-1, keepdims=True) + EPS) * g_ref[...]
    o_ref[...] = x2


def _resid(x1, yt, gt2, final_g, seq, final):
    t, d = x1.shape
    tm = min(256, seq)
    per_b = seq // tm
    per_p = yt.shape[2] // tm
    return pl.pallas_call(
        functools.partial(_resid_kernel, final=final),
        name="peer_resid",
        grid=(t // tm,),
        in_specs=[
            pl.BlockSpec((tm, d), lambda i: (i, 0)),
            pl.BlockSpec((None, d, tm), lambda i: (i // per_p, 0, i % per_p)),
            pl.BlockSpec((None, 1, d), lambda i: (i // per_b, 0, 0)),
            pl.BlockSpec((1, d), lambda i: (0, 0)),
        ],
        out_specs=pl.BlockSpec((tm, d), lambda i: (i, 0)),
        out_shape=jax.ShapeDtypeStruct((t, d), F32),
        compiler_params=_cparams(("parallel",)),
    )(x1, yt, gt2, final_g)


def _layout_w_in(w_in_l):
    d = w_in_l.shape[0]
    o_a = 2 * GLA_QK + 2 * GLA_V
    o_u = o_a + GLA_RANK
    pad = jnp.zeros((d, PROJ_COLS - (w_in_l.shape[1] - GLA_RANK) - GLA_RANK), w_in_l.dtype)
    w = jnp.concatenate([w_in_l[:, :o_a], w_in_l[:, o_u:], w_in_l[:, o_a:o_u], pad], axis=1)
    w = w.astype(BF16).reshape(d, PROJ_COLS // PROJ_TN, PROJ_TN)
    return w.transpose(1, 0, 2)


def kernel(x, c, ada_w, ada_b, norm1_g, w_in, gla_w_a2, gla_b_a, gla_norm_g, gmlp_vnorm_g,
           gmlp_ws, gmlp_b, gmlp_out_g, w_out, norm2_g, peer_wq, peer_k1, peer_k2, peer_u,
           peer_v, final_g):
    batch, seq, d = x.shape
    depth = ada_w.shape[0]
    t = batch * seq
    xf = x.reshape(t, d)

    mod = _modulation(c, ada_w, ada_b)
    causal = jnp.tril(jnp.ones((GMLP_CHUNK, GMLP_CHUNK), F32))

    for l in range(depth):
        sh1, sc1, gt1, sh2, sc2, gt2 = [m.reshape(batch, 1, d) for m in jnp.split(mod[l], 6, axis=-1)]

        proj = _inproj(xf, norm1_g[l].reshape(1, d), sc1, sh1, _layout_w_in(w_in[l]), seq)

        wa2p = jnp.zeros((LANES, GLA_QK), F32).at[:GLA_RANK].set(gla_w_a2[l])
        y_gla = _gla(proj.reshape(batch, seq, PROJ_COLS), wa2p, gla_b_a[l].reshape(1, GLA_QK),
                     gla_norm_g[l].reshape(1, GLA_V), batch, seq).reshape(t, GLA_V)

        bias_full = jnp.repeat(gmlp_b[l].T, GMLP_DH, axis=1)
        x1, h2, h2t = _outproj(
            y_gla, proj, (gmlp_ws[l] * causal).astype(BF16), bias_full,
            gmlp_vnorm_g[l].reshape(1, GMLP_WIDTH), gmlp_out_g[l].reshape(1, GMLP_WIDTH),
            w_out[l].astype(BF16), xf, gt1, norm2_g[l].reshape(1, d), sc2, sh2, seq)

        rank2, p2, n1, p1 = _route(h2, peer_wq[l].astype(BF16), peer_k1[l].astype(BF16),
                                   peer_k2[l].astype(BF16))
        pvt = peer_v[l].reshape(N_EXPERTS // PEER_TE, PEER_TE, d).transpose(0, 2, 1).astype(BF16)
        yt = _peer_dense(h2t, peer_u[l].astype(BF16), pvt, rank2, p2, n1, p1)
        xf = _resid(x1, yt, gt2, final_g.reshape(1, d), seq, final=(l == depth - 1))

    return xf.reshape(batch, seq, d)
```

```python
import functools

import jax
import jax.numpy as jnp
from jax import lax
from jax.experimental import pallas as pl
from jax.experimental.pallas import tpu as pltpu

F32 = jnp.float32
BF16 = jnp.bfloat16

D_MODEL = 2048
EPS = 1e-6
GLA_HEADS = 4
GLA_DV = 256
GLA_DK = 128
GLA_QK = GLA_HEADS * GLA_DK
GLA_V = GLA_HEADS * GLA_DV
GLA_RANK = 16
GLA_CHUNK = 64
GMLP_HEADS = 8
GMLP_WIDTH = 1024
GMLP_DH = 128
GMLP_CHUNK = 128
N_KEYS = 128
N_EXPERTS = N_KEYS * N_KEYS
PEER_HEADS = 8
PEER_TOPK = 16
PEER_DQ = 256

LANES = 128
SUBLANES = 8
PROJ_COLS = 5376
COL_U = 3
COL_VSP = 4
COL_A = 40
VMEM_LIMIT = 56 * 1024 * 1024

NEG_INF = float("-inf")


def _cparams(sem):
    return pltpu.CompilerParams(dimension_semantics=sem, vmem_limit_bytes=VMEM_LIMIT)


def _gelu(x):
    c = 0.7978845608028654
    return 0.5 * x * (1.0 + jnp.tanh(c * (x + 0.044715 * (x * x * x))))


def _split3(x):
    hi = x.astype(BF16)
    r1 = x - hi.astype(F32)
    mid = r1.astype(BF16)
    lo = (r1 - mid.astype(F32)).astype(BF16)
    return hi, mid, lo


def _dot(a, b):
    return jnp.dot(a, b, preferred_element_type=F32)


def _dot_nt(a, b):
    return lax.dot_general(a, b, (((1,), (1,)), ((), ())), preferred_element_type=F32)


def _dot_tn(a, b):
    return lax.dot_general(a, b, (((0,), (0,)), ((), ())), preferred_element_type=F32)


def _mod_kernel(c_ref, w_ref, b_ref, o_ref):
    c = c_ref[...]
    cond = c / (1.0 + jnp.exp(-c))
    acc = jnp.zeros(o_ref.shape, F32)
    w = w_ref[...]
    w_parts = _split3(w)
    c_parts = _split3(cond)
    for ci, wi in ((0, 0), (0, 1), (1, 0), (0, 2), (1, 1), (2, 0)):
        acc = acc + _dot(c_parts[ci], w_parts[wi])
    o_ref[...] = acc + b_ref[...]


def _modulation(c, ada_w, ada_b):
    depth, d, n = ada_w.shape
    b = c.shape[0]
    rows = 16
    cpad = jnp.zeros((rows, d), F32).at[:b].set(c)
    tn = 512
    out = pl.pallas_call(
        _mod_kernel,
        name="adaln_mod",
        grid=(depth, n // tn),
        in_specs=[
            pl.BlockSpec((rows, d), lambda l, j: (0, 0)),
            pl.BlockSpec((None, d, tn), lambda l, j: (l, 0, j)),
            pl.BlockSpec((None, 1, tn), lambda l, j: (l, 0, j)),
        ],
        out_specs=pl.BlockSpec((None, rows, tn), lambda l, j: (l, 0, j)),
        out_shape=jax.ShapeDtypeStruct((depth, rows, n), F32),
        compiler_params=_cparams(("parallel", "parallel")),
    )(cpad, ada_w, ada_b.reshape(depth, 1, n))
    return out[:, :b]


def _inproj_kernel(x_ref, g_ref, sc_ref, sh_ref, w_ref, o_ref, h_scr):
    @pl.when(pl.program_id(1) == 0)
    def _():
        x = x_ref[...]
        ms = jnp.mean(x * x, axis=-1, keepdims=True)
        y = x * lax.rsqrt(ms + EPS) * g_ref[...]
        h_scr[...] = (y * (1.0 + sc_ref[...]) + sh_ref[...]).astype(BF16)

    o_ref[...] = _dot(h_scr[...], w_ref[pl.program_id(1)])


PROJ_TN = 1792


def _inproj(x, g, sc, sh, w, seq):
    t, d = x.shape
    n_tiles, _, tn = w.shape
    n = n_tiles * tn
    tm = min(512, seq)
    per_b = seq // tm
    return pl.pallas_call(
        _inproj_kernel,
        name="norm_inproj",
        grid=(t // tm, n_tiles),
        in_specs=[
            pl.BlockSpec((tm, d), lambda i, j: (i, 0)),
            pl.BlockSpec((1, d), lambda i, j: (0, 0)),
            pl.BlockSpec((None, 1, d), lambda i, j: (i // per_b, 0, 0)),
            pl.BlockSpec((None, 1, d), lambda i, j: (i // per_b, 0, 0)),
            pl.BlockSpec(w.shape, lambda i, j: (0, 0, 0), pipeline_mode=pl.Buffered(1)),
        ],
        out_specs=pl.BlockSpec((tm, tn), lambda i, j: (i, j)),
        out_shape=jax.ShapeDtypeStruct((t, n), F32),
        scratch_shapes=[pltpu.VMEM((tm, d), BF16)],
        compiler_params=_cparams(("parallel", "arbitrary")),
    )(x, g, sc, sh, w)


GLA_LEVELS = (32, 16, 8, 4, 2, 1)


def _gla_kernel(q_ref, k_ref, v_ref, r_ref, a_ref, wa_ref, ba_ref, g_ref, o_ref,
                state_ref, *, n_chunks):
    C = GLA_CHUNK

    @pl.when(pl.program_id(1) == 0)
    def _():
        state_ref[...] = jnp.zeros(state_ref.shape, F32)

    nl = len(GLA_LEVELS) + 1
    ri = lax.broadcasted_iota(jnp.int32, (C, C), 0)
    ci = lax.broadcasted_iota(jnp.int32, (C, C), 1)
    pieces = [ci <= ri]
    masks = [ci == ri]
    for s in GLA_LEVELS:
        blk_r = ri // (2 * s)
        bound = blk_r * (2 * s) + (s - 1)
        pieces.append(ci <= bound)
        masks.append((blk_r == ci // (2 * s)) & (ri % (2 * s) >= s) & (ci % (2 * s) < s))
    prefix = jnp.concatenate([jnp.where(p, 1.0, 0.0).astype(BF16) for p in pieces], axis=0)

    wa = wa_ref[...].astype(BF16)
    ba = ba_ref[...]
    gain = g_ref[...]
    scale = GLA_DK ** -0.5

    def chunk(c, carry):
        off = pl.multiple_of(c * C, C)
        rows = pl.ds(off, C)
        for b in range(q_ref.shape[0]):
            z = _dot(a_ref[b, rows, :].astype(BF16), wa) + ba
            la = -(jnp.maximum(-z, 0.0) + jnp.log1p(jnp.exp(-jnp.abs(z)))) * (1.0 / 16.0)
            hi, mid, lo = _split3(la)
            pref_all = _dot(prefix, hi) + _dot(prefix, mid) + _dot(prefix, lo)

            for h in range(GLA_HEADS):
                kcols = slice(h * GLA_DK, (h + 1) * GLA_DK)
                vcols = slice(h * GLA_DV, (h + 1) * GLA_DV)
                pref = pref_all[:, kcols]
                q = q_ref[b, rows, kcols] * scale
                k = k_ref[b, rows, kcols]
                v = v_ref[b, rows, vcols].astype(BF16)
                G = pref[0:C]
                g_last = G[C - 1:C, :]

                attn = jnp.where(masks[0], _dot_nt(q.astype(BF16), k.astype(BF16)), 0.0)
                for l in range(1, nl):
                    e = jnp.exp(-jnp.abs(G - pref[l * C:(l + 1) * C]))
                    a_l = _dot_nt((q * e).astype(BF16), (k * e).astype(BF16))
                    attn = attn + jnp.where(masks[l], a_l, 0.0)

                st = state_ref[b, h]
                o = (_dot(attn.astype(BF16), v)
                     + _dot_nt((q * jnp.exp(G)).astype(BF16), st.astype(BF16)))
                k_dec = (k * jnp.exp(g_last - G)).astype(BF16)
                state_ref[b, h] = st * jnp.exp(g_last) + _dot_tn(v, k_dec)

                y = o * lax.rsqrt(jnp.mean(o * o, axis=-1, keepdims=True) + EPS) * gain[:, vcols]
                r = r_ref[b, rows, vcols]
                o_ref[b, rows, vcols] = (y * (r / (1.0 + jnp.exp(-r)))).astype(o_ref.dtype)
        return carry

    lax.fori_loop(0, n_chunks, chunk, 0, unroll=2)


def _gla(proj, wa2p, ba, norm_g, batch, seq):
    lc = min(512, seq)
    nb = 2 if batch % 2 == 0 else 1
    blk = lambda width, col: pl.BlockSpec((nb, lc, width), lambda b, s: (b, s, col))
    return pl.pallas_call(
        functools.partial(_gla_kernel, n_chunks=lc // GLA_CHUNK),
        name="gla",
        grid=(batch // nb, seq // lc),
        in_specs=[
            blk(GLA_QK, 0),
            blk(GLA_QK, 1),
            blk(GLA_V, 1),
            blk(GLA_V, 2),
            blk(LANES, COL_A),
            pl.BlockSpec((LANES, GLA_QK), lambda b, s: (0, 0)),
            pl.BlockSpec((1, GLA_QK), lambda b, s: (0, 0)),
            pl.BlockSpec((1, GLA_V), lambda b, s: (0, 0)),
        ],
        out_specs=blk(GLA_V, 0),
        out_shape=jax.ShapeDtypeStruct((batch, seq, GLA_V), BF16),
        scratch_shapes=[pltpu.VMEM((nb, GLA_HEADS, GLA_DV, GLA_DK), F32)],
        compiler_params=_cparams(("parallel", "arbitrary")),
    )(proj, proj, proj, proj, proj, wa2p, ba, norm_g)


def _gmlp_chunk(u_ref, v_ref, ws_ref, bias_ref, vg_ref, og_ref, ym_scr, rows):
    for h in range(GMLP_HEADS):
        cols = slice(h * GMLP_DH, (h + 1) * GMLP_DH)
        v = _gelu(v_ref[rows, cols])
        v = v * lax.rsqrt(jnp.mean(v * v, axis=-1, keepdims=True) + EPS) * vg_ref[:, cols]
        sv = _dot(ws_ref[h], v.astype(BF16)) + bias_ref[:, cols]
        y = _gelu(u_ref[rows, cols]) * sv
        y = y * lax.rsqrt(jnp.mean(y * y, axis=-1, keepdims=True) + EPS) * og_ref[:, cols]
        ym_scr[rows, cols] = y.astype(ym_scr.dtype)


def _outproj_kernel(yg_ref, u_ref, v_ref, ws_ref, bias_ref, vg_ref, og_ref, wg_ref, wm_ref,
                    x_ref, gt_ref, g2_ref, sc_ref, sh_ref, x1_ref, h2_ref, h2t_ref, ym_scr):
    mix = _dot(yg_ref[...], wg_ref[...])
    for c in range(yg_ref.shape[0] // GMLP_CHUNK):
        _gmlp_chunk(u_ref, v_ref, ws_ref, bias_ref, vg_ref, og_ref, ym_scr,
                    slice(c * GMLP_CHUNK, (c + 1) * GMLP_CHUNK))
    mix = mix + _dot(ym_scr[...], wm_ref[...])
    x1 = x_ref[...] + gt_ref[...] * mix
    x1_ref[...] = x1
    y = x1 * lax.rsqrt(jnp.mean(x1 * x1, axis=-1, keepdims=True) + EPS) * g2_ref[...]
    h2 = y * (1.0 + sc_ref[...]) + sh_ref[...]
    h2_ref[...] = h2.astype(BF16)
    h2t_ref[...] = h2.T.astype(BF16)


def _peer_tm(t):
    return min(512, t)


def _outproj(yg, proj, ws_causal, bias_full, vnorm_g, out_g, w_out, x, gt1, g2, sc2, sh2, seq):
    t, d = x.shape
    tm = min(256, seq)
    per_b = seq // tm
    half = d // 2
    ptm = _peer_tm(t)
    per_p = ptm // tm
    c = GMLP_CHUNK
    bvec = pl.BlockSpec((None, 1, d), lambda i: (i // per_b, 0, 0))
    return pl.pallas_call(
        _outproj_kernel,
        name="outproj",
        grid=(t // tm,),
        in_specs=[
            pl.BlockSpec((tm, half), lambda i: (i, 0)),
            pl.BlockSpec((tm, GMLP_WIDTH), lambda i: (i, COL_U)),
            pl.BlockSpec((tm, GMLP_WIDTH), lambda i: (i, COL_VSP)),
            pl.BlockSpec((GMLP_HEADS, c, c), lambda i: (0, 0, 0)),
            pl.BlockSpec((c, GMLP_WIDTH), lambda i: (0, 0)),
            pl.BlockSpec((1, GMLP_WIDTH), lambda i: (0, 0)),
            pl.BlockSpec((1, GMLP_WIDTH), lambda i: (0, 0)),
            pl.BlockSpec((half, d), lambda i: (0, 0)),
            pl.BlockSpec((half, d), lambda i: (1, 0)),
            pl.BlockSpec((tm, d), lambda i: (i, 0)),
            bvec,
            pl.BlockSpec((1, d), lambda i: (0, 0)),
            bvec,
            bvec,
        ],
        out_specs=[pl.BlockSpec((tm, d), lambda i: (i, 0)),
                   pl.BlockSpec((tm, d), lambda i: (i, 0)),
                   pl.BlockSpec((None, d, tm), lambda i: (i // per_p, 0, i % per_p))],
        out_shape=[jax.ShapeDtypeStruct((t, d), F32), jax.ShapeDtypeStruct((t, d), BF16),
                   jax.ShapeDtypeStruct((t // ptm, d, ptm), BF16)],
        scratch_shapes=[pltpu.VMEM((tm, GMLP_WIDTH), BF16)],
        compiler_params=_cparams(("parallel",)),
    )(yg, proj, proj, ws_causal, bias_full, vnorm_g, out_g, w_out, w_out, x, gt1, g2, sc2, sh2)


CAND_COLS = tuple(PEER_TOPK // (r + 1) for r in range(PEER_TOPK))
BIG_IDX = float(1 << 20)


def _top16_exact(s, vals_ref):
    key = lax.broadcasted_iota(jnp.int32, s.shape, 0).astype(F32)
    rank = jnp.full(s.shape, float(PEER_TOPK), F32)
    for r in range(PEER_TOPK):
        m = jnp.max(s, axis=0, keepdims=True)
        first = jnp.min(jnp.where(s == m, key, BIG_IDX), axis=0, keepdims=True)
        sel = key == first
        rank = jnp.where(sel, float(r), rank)
        s = jnp.where(sel, NEG_INF, s)
        vals_ref[r:r + 1, :] = m
    return rank


def _batcher_pairs(n):
    def merge(lo, hi, r):
        step = r * 2
        if step < hi - lo:
            yield from merge(lo, hi, step)
            yield from merge(lo + r, hi, step)
            yield from ((i, i + r) for i in range(lo + r, hi - r, step))
        else:
            yield (lo, lo + r)

    def sort(lo, hi):
        if hi - lo >= 1:
            mid = lo + (hi - lo) // 2
            yield from sort(lo, mid)
            yield from sort(mid + 1, hi)
            yield from merge(lo, hi, 1)

    return tuple(sort(0, n - 1))


SORT16_PAIRS = _batcher_pairs(PEER_TOPK)


def _top16_sorted(s, vals_ref):
    n = PEER_TOPK
    v = [s[i * SUBLANES:(i + 1) * SUBLANES, :] for i in range(n)]
    for i, j in SORT16_PAIRS:
        v[i], v[j] = jnp.maximum(v[i], v[j]), jnp.minimum(v[i], v[j])
    for shift in (4, 2, 1):
        v = [jnp.maximum(v[i], pltpu.roll(v[n - 1 - i], SUBLANES - shift, axis=0))
             for i in range(n)]
        for dist in (8, 4, 2, 1):
            for i in range(n):
                if i % (2 * dist) < dist:
                    j = i + dist
                    v[i], v[j] = jnp.maximum(v[i], v[j]), jnp.minimum(v[i], v[j])
    rows = [x[0:1, :] for x in v]
    for r in range(n):
        vals_ref[r:r + 1, :] = rows[r]
    count = jnp.sum(jnp.where(s >= rows[n - 1], 1.0, 0.0), axis=0, keepdims=True)
    for r in range(n - 1):
        count = count + jnp.where(rows[r] == rows[r + 1], 1.0, 0.0)
    return count


def _route_tile(s1, s2, v1_scr, v2_scr, ties):
    width = s1.shape[1]
    sub = lax.broadcasted_iota(jnp.int32, (SUBLANES, width), 0).astype(F32)
    if ties:
        rank1 = _top16_exact(s1, v1_scr)
        rank2 = _top16_exact(s2, v2_scr)
        marked1 = marked2 = jnp.full((1, width), float(PEER_TOPK), F32)
    else:
        marked1 = _top16_sorted(s1, v1_scr)
        marked2 = _top16_sorted(s2, v2_scr)
        rank2 = jnp.zeros(s2.shape, F32)
        for r in range(PEER_TOPK):
            rank2 = jnp.where(v2_scr[r:r + 1, :] > s2, float(r + 1), rank2)

    v2a = v2_scr[0:8, :]
    v2b = v2_scr[8:16, :]
    cands = [v1_scr[0:1, :] + v2a, v1_scr[0:1, :] + v2b]
    ids = [sub, sub + 8]
    for r in range(1, 8):
        cands.append(jnp.where(sub < CAND_COLS[r], v1_scr[r:r + 1, :] + v2a, NEG_INF))
        ids.append(sub + PEER_TOPK * r)
    cands.append(v1_scr[8:16, :] + v2_scr[0:1, :])
    ids.append((sub + 8) * PEER_TOPK)
    taken = [jnp.zeros((SUBLANES, width), F32) for _ in cands]
    mx = v1_scr[0:1, :] + v2_scr[0:1, :]
    zsum = jnp.zeros((1, width), F32)
    for _ in range(PEER_TOPK):
        m = functools.reduce(jnp.maximum, cands)
        m = jnp.max(m, axis=0, keepdims=True)
        sels = [c == m for c in cands]
        if ties:
            first = functools.reduce(
                jnp.minimum, [jnp.where(s, i, BIG_IDX) for s, i in zip(sels, ids)])
            first = jnp.min(first, axis=0, keepdims=True)
            sels = [i == first for i in ids]
        cands = [jnp.where(s, NEG_INF, c) for s, c in zip(sels, cands)]
        taken = [jnp.where(s, 1.0, t) for s, t in zip(sels, taken)]
        zsum = zsum + jnp.exp(m - mx)

    counts = [jnp.sum(taken[0] + taken[1], axis=0, keepdims=True)]
    for r in range(1, 8):
        counts.append(jnp.sum(taken[r + 1], axis=0, keepdims=True))
    for r in range(8, PEER_TOPK):
        counts.append(taken[9][r - 8:r - 7, :])
    n1 = jnp.zeros(s1.shape, F32)
    for r in range(PEER_TOPK):
        hit = (rank1 == r) if ties else (s1 == v1_scr[r:r + 1, :])
        n1 = jnp.where(hit, counts[r], n1)

    p1 = jnp.exp(s1 - v1_scr[0:1, :]) * (1.0 / zsum)
    p2 = jnp.exp(s2 - v2_scr[0:1, :])
    return n1, p1, rank2, p2, [marked1, marked2, functools.reduce(jnp.add, counts)]


def _route_kernel(h_ref, wq_ref, k1_ref, k2_ref, rank2_ref, p2_ref, n1_ref, p1_ref,
                  s1_scr, s2_scr, v1_scr, v2_scr):
    half = PEER_DQ // 2

    def scores(h):
        q = _dot(h_ref[...], wq_ref[h])
        s1_scr[h] = _dot_nt(k1_ref[h], q[:, :half].astype(BF16))
        s2_scr[h] = _dot_nt(k2_ref[h], q[:, half:].astype(BF16))

    scores(0)

    def body(h, carry):
        s1 = s1_scr[h]
        s2 = s2_scr[h]
        scores(jnp.minimum(h + 1, PEER_HEADS - 1))

        def run(ties):
            n1, p1, rank2, p2, marked = _route_tile(s1, s2, v1_scr, v2_scr, ties)
            n1_ref[h] = n1
            p1_ref[h] = p1
            rank2_ref[h] = rank2.astype(rank2_ref.dtype)
            p2_ref[h] = p2.astype(p2_ref.dtype)
            return marked

        marked = run(ties=False)
        bad = functools.reduce(
            jnp.maximum, [jnp.where(mk == PEER_TOPK, 0.0, 1.0) for mk in marked])

        @pl.when(jnp.max(bad) > 0.0)
        def _():
            run(ties=True)

        return carry

    lax.fori_loop(0, PEER_HEADS, body, 0)


def _route(h2, wq, k1, k2):
    t, d = h2.shape
    tm = _peer_tm(t)
    hk = (PEER_HEADS, N_KEYS, tm)
    tiled = (t // tm,) + hk
    out_spec = pl.BlockSpec((None,) + hk, lambda i: (i, 0, 0, 0))
    kspec = pl.BlockSpec((PEER_HEADS, N_KEYS, PEER_DQ // 2), lambda i: (0, 0, 0))
    wq_heads = wq.reshape(d, PEER_HEADS, PEER_DQ).transpose(1, 0, 2)
    return pl.pallas_call(
        _route_kernel,
        name="peer_route",
        grid=(t // tm,),
        in_specs=[
            pl.BlockSpec((tm, d), lambda i: (i, 0)),
            pl.BlockSpec(wq_heads.shape, lambda i: (0, 0, 0)),
            kspec, kspec,
        ],
        out_specs=[out_spec, out_spec, out_spec, out_spec],
        out_shape=[
            jax.ShapeDtypeStruct(tiled, BF16),
            jax.ShapeDtypeStruct(tiled, BF16),
            jax.ShapeDtypeStruct(tiled, F32),
            jax.ShapeDtypeStruct(tiled, F32),
        ],
        scratch_shapes=[
            pltpu.VMEM(hk, F32), pltpu.VMEM(hk, F32),
            pltpu.VMEM((PEER_TOPK, tm), F32), pltpu.VMEM((PEER_TOPK, tm), F32),
        ],
        compiler_params=_cparams(("parallel",)),
    )(h2, wq_heads, k1, k2)


def _peer_kernel(ht_ref, pu_ref, pvt_ref, rank2_ref, p2_ref, n1_ref, p1_ref, o_ref,
                 pa_scr, pb_scr, *, a_rows, n_e):
    j = pl.program_id(1)
    group = 1
    acc_split = 1
    n_groups = a_rows // group
    o_rows = o_ref.shape[0] // n_groups

    def activations(k, p_scr):
        base = pl.multiple_of(k * (group * N_KEYS), group * N_KEYS)
        scores = _dot(pu_ref[pl.ds(base, group * N_KEYS), :], ht_ref[...])
        for g in range(group):
            a = j * a_rows + k * group + g
            w = None
            for h in range(PEER_HEADS):
                n1 = n1_ref[h, pl.ds(a, 1), :].astype(BF16)
                p1 = p1_ref[h, pl.ds(a, 1), :].astype(BF16)
                term = jnp.where(rank2_ref[h] < n1, p2_ref[h], jnp.zeros((), BF16)) * p1
                w = term if w is None else w + term
            act = _gelu(scores[g * N_KEYS:(g + 1) * N_KEYS, :])
            p_scr[pl.ds(base + g * N_KEYS, N_KEYS), :] = (act * w.astype(F32)).astype(BF16)

    def accumulate(k, p_scr):
        piece = o_rows // acc_split
        for s in range(acc_split):
            rows = pl.ds(pl.multiple_of(k * o_rows + s * piece, piece), piece)
            o_ref[rows, :] += _dot(pvt_ref[rows, :], p_scr[...])

    def both(p_new, p_old):
        def body(k, carry):
            activations(k, p_new)
            accumulate(k, p_old)
            return carry
        lax.fori_loop(0, n_groups, body, 0, unroll=n_groups)

    @pl.when(j == 0)
    def _():
        o_ref[...] = jnp.zeros(o_ref.shape, F32)
        lax.fori_loop(0, n_groups, lambda k, c: (activations(k, pa_scr), c)[1], 0)

    @pl.when((j > 0) & (j < n_e) & (j % 2 == 0))
    def _():
        both(pa_scr, pb_scr)

    @pl.when((j < n_e) & (j % 2 == 1))
    def _():
        both(pb_scr, pa_scr)

    @pl.when(j == n_e)
    def _():
        p_last = pb_scr if n_e % 2 == 0 else pa_scr
        lax.fori_loop(0, n_groups, lambda k, c: (accumulate(k, p_last), c)[1], 0)


PEER_TE = 1024


def _peer_dense(ht, pu, pvt, rank2, p2, n1, p1):
    n_t, d, tm = ht.shape
    te = PEER_TE
    n_e = N_EXPERTS // te
    rspec = pl.BlockSpec((None, PEER_HEADS, N_KEYS, tm), lambda i, j: (i, 0, 0, 0))
    return pl.pallas_call(
        functools.partial(_peer_kernel, a_rows=te // N_KEYS, n_e=n_e),
        name="peer_dense",
        grid=(n_t, n_e + 1),
        in_specs=[
            pl.BlockSpec((None, d, tm), lambda i, j: (i, 0, 0)),
            pl.BlockSpec((te, d), lambda i, j: (jnp.minimum(j, n_e - 1), 0)),
            pl.BlockSpec((None, d, te), lambda i, j: (jnp.maximum(j - 1, 0), 0, 0)),
            rspec, rspec, rspec, rspec,
        ],
        out_specs=pl.BlockSpec((None, d, tm), lambda i, j: (i, 0, 0)),
        out_shape=jax.ShapeDtypeStruct((n_t, d, tm), F32),
        scratch_shapes=[pltpu.VMEM((te, tm), BF16), pltpu.VMEM((te, tm), BF16)],
        compiler_params=_cparams(("parallel", "arbitrary")),
    )(ht, pu, pvt, rank2, p2, n1, p1)


def _resid_kernel(x_ref, yt_ref, gt_ref, g_ref, o_ref, *, final):
    x2 = x_ref[...] + gt_ref[...] * yt_ref[...].T
    if final:
        x2 = x2 * lax.rsqrt(jnp.mean(x2 * x2, axis=-1, keepdims=True) + EPS) * g_ref[...]
    o_ref[...] = x2


def _resid(x1, yt, gt2, final_g, seq, final):
    t, d = x1.shape
    tm = min(256, seq)
    per_b = seq // tm
    per_p = yt.shape[2] // tm
    return pl.pallas_call(
        functools.partial(_resid_kernel, final=final),
        name="peer_resid",
        grid=(t // tm,),
        in_specs=[
            pl.BlockSpec((tm, d), lambda i: (i, 0)),
            pl.BlockSpec((None, d, tm), lambda i: (i // per_p, 0, i % per_p)),
            pl.BlockSpec((None, 1, d), lambda i: (i // per_b, 0, 0)),
            pl.BlockSpec((1, d), lambda i: (0, 0)),
        ],
        out_specs=pl.BlockSpec((tm, d), lambda i: (i, 0)),
        out_shape=jax.ShapeDtypeStruct((t, d), F32),
        compiler_params=_cparams(("parallel",)),
    )(x1, yt, gt2, final_g)


def _layout_w_in(w_in_l):
    d = w_in_l.shape[0]
    o_a = 2 * GLA_QK + 2 * GLA_V
    o_u = o_a + GLA_RANK
    pad = jnp.zeros((d, PROJ_COLS - (w_in_l.shape[1] - GLA_RANK) - GLA_RANK), w_in_l.dtype)
    w = jnp.concatenate([w_in_l[:, :o_a], w_in_l[:, o_u:], w_in_l[:, o_a:o_u], pad], axis=1)
    w = w.astype(BF16).reshape(d, PROJ_COLS // PROJ_TN, PROJ_TN)
    return w.transpose(1, 0, 2)


def kernel(x, c, ada_w, ada_b, norm1_g, w_in, gla_w_a2, gla_b_a, gla_norm_g, gmlp_vnorm_g,
           gmlp_ws, gmlp_b, gmlp_out_g, w_out, norm2_g, peer_wq, peer_k1, peer_k2, peer_u,
           peer_v, final_g):
    batch, seq, d = x.shape
    depth = ada_w.shape[0]
    t = batch * seq
    xf = x.reshape(t, d)

    mod = _modulation(c, ada_w, ada_b)
    causal = jnp.tril(jnp.ones((GMLP_CHUNK, GMLP_CHUNK), F32))

    for l in range(depth):
        sh1, sc1, gt1, sh2, sc2, gt2 = [m.reshape(batch, 1, d) for m in jnp.split(mod[l], 6, axis=-1)]

        proj = _inproj(xf, norm1_g[l].reshape(1, d), sc1, sh1, _layout_w_in(w_in[l]), seq)

        wa2p = jnp.zeros((LANES, GLA_QK), F32).at[:GLA_RANK].set(gla_w_a2[l])
        y_gla = _gla(proj.reshape(batch, seq, PROJ_COLS), wa2p, gla_b_a[l].reshape(1, GLA_QK),
                     gla_norm_g[l].reshape(1, GLA_V), batch, seq).reshape(t, GLA_V)

        bias_full = jnp.repeat(gmlp_b[l].T, GMLP_DH, axis=1)
        x1, h2, h2t = _outproj(
            y_gla, proj, (gmlp_ws[l] * causal).astype(BF16), bias_full,
            gmlp_vnorm_g[l].reshape(1, GMLP_WIDTH), gmlp_out_g[l].reshape(1, GMLP_WIDTH),
            w_out[l].astype(BF16), xf, gt1, norm2_g[l].reshape(1, d), sc2, sh2, seq)

        rank2, p2, n1, p1 = _route(h2, peer_wq[l].astype(BF16), peer_k1[l].astype(BF16),
                                   peer_k2[l].astype(BF16))
        pvt = peer_v[l].reshape(N_EXPERTS // PEER_TE, PEER_TE, d).transpose(0, 2, 1).astype(BF16)
        yt = _peer_dense(h2t, peer_u[l].astype(BF16), pvt, rank2, p2, n1, p1)
        xf = _resid(x1, yt, gt2, final_g.reshape(1, d), seq, final=(l == depth - 1))

    return xf.reshape(batch, seq, d)
```

```python
import functools

import jax
import jax.numpy as jnp
from jax import lax
from jax.experimental import pallas as pl
from jax.experimental.pallas import tpu as pltpu

F32 = jnp.float32
BF16 = jnp.bfloat16

D_MODEL = 2048
EPS = 1e-6
GLA_HEADS = 4
GLA_DV = 256
GLA_DK = 128
GLA_QK = GLA_HEADS * GLA_DK
GLA_V = GLA_HEADS * GLA_DV
GLA_RANK = 16
GLA_CHUNK = 64
GMLP_HEADS = 8
GMLP_WIDTH = 1024
GMLP_DH = 128
GMLP_CHUNK = 128
N_KEYS = 128
N_EXPERTS = N_KEYS * N_KEYS
PEER_HEADS = 8
PEER_TOPK = 16
PEER_DQ = 256

LANES = 128
SUBLANES = 8
PROJ_COLS = 5376
COL_U = 3
COL_VSP = 4
COL_A = 40
VMEM_LIMIT = 56 * 1024 * 1024

NEG_INF = float("-inf")


def _cparams(sem):
    return pltpu.CompilerParams(dimension_semantics=sem, vmem_limit_bytes=VMEM_LIMIT)


def _gelu(x):
    c = 0.7978845608028654
    return 0.5 * x * (1.0 + jnp.tanh(c * (x + 0.044715 * (x * x * x))))


def _split3(x):
    hi = x.astype(BF16)
    r1 = x - hi.astype(F32)
    mid = r1.astype(BF16)
    lo = (r1 - mid.astype(F32)).astype(BF16)
    return hi, mid, lo


def _dot(a, b):
    return jnp.dot(a, b, preferred_element_type=F32)


def _dot_nt(a, b):
    return lax.dot_general(a, b, (((1,), (1,)), ((), ())), preferred_element_type=F32)


def _dot_tn(a, b):
    return lax.dot_general(a, b, (((0,), (0,)), ((), ())), preferred_element_type=F32)


def _mod_kernel(cb_ref, w_ref, b_ref, o_ref, cond_scr):
    @pl.when((pl.program_id(0) == 0) & (pl.program_id(1) == 0))
    def _():
        c = cb_ref[...]
        cond_scr[...] = c / (1.0 + jnp.exp(-c))

    tn = w_ref.shape[1]
    for b in range(cb_ref.shape[0]):
        cond = cond_scr[b]
        cols = [jnp.sum(w_ref[:, j * LANES:(j + 1) * LANES] * cond, axis=0, keepdims=True)
                for j in range(tn // LANES)]
        o_ref[b:b + 1, :] = jnp.concatenate(cols, axis=1) + b_ref[...]


def _modulation(c, ada_w, ada_b):
    depth, d, n = ada_w.shape
    b = c.shape[0]
    tn = 512
    cb = jnp.broadcast_to(c[:, :, None], (b, d, LANES))
    return pl.pallas_call(
        _mod_kernel,
        name="adaln_mod",
        grid=(depth, n // tn),
        in_specs=[
            pl.BlockSpec((b, d, LANES), lambda l, j: (0, 0, 0)),
            pl.BlockSpec((None, d, tn), lambda l, j: (l, 0, j)),
            pl.BlockSpec((None, 1, tn), lambda l, j: (l, 0, j)),
        ],
        out_specs=pl.BlockSpec((None, b, tn), lambda l, j: (l, 0, j)),
        out_shape=jax.ShapeDtypeStruct((depth, b, n), F32),
        scratch_shapes=[pltpu.VMEM((b, d, LANES), F32)],
        compiler_params=_cparams(("arbitrary", "arbitrary")),
    )(cb, ada_w, ada_b.reshape(depth, 1, n))


def _inproj_kernel(x_ref, g_ref, sc_ref, sh_ref, w_ref, o_ref, h_scr):
    @pl.when(pl.program_id(1) == 0)
    def _():
        x = x_ref[...]
        ms = jnp.mean(x * x, axis=-1, keepdims=True)
        y = x * lax.rsqrt(ms + EPS) * g_ref[...]
        h_scr[...] = (y * (1.0 + sc_ref[...]) + sh_ref[...]).astype(BF16)

    o_ref[...] = _dot(h_scr[...], w_ref[pl.program_id(1)])


PROJ_TN = 1792


def _inproj(x, g, sc, sh, w, seq):
    t, d = x.shape
    n_tiles, _, tn = w.shape
    n = n_tiles * tn
    tm = min(512, seq)
    per_b = seq // tm
    return pl.pallas_call(
        _inproj_kernel,
        name="norm_inproj",
        grid=(t // tm, n_tiles),
        in_specs=[
            pl.BlockSpec((tm, d), lambda i, j: (i, 0)),
            pl.BlockSpec((1, d), lambda i, j: (0, 0)),
            pl.BlockSpec((None, 1, d), lambda i, j: (i // per_b, 0, 0)),
            pl.BlockSpec((None, 1, d), lambda i, j: (i // per_b, 0, 0)),
            pl.BlockSpec(w.shape, lambda i, j: (0, 0, 0), pipeline_mode=pl.Buffered(1)),
        ],
        out_specs=pl.BlockSpec((tm, tn), lambda i, j: (i, j)),
        out_shape=jax.ShapeDtypeStruct((t, n), F32),
        scratch_shapes=[pltpu.VMEM((tm, d), BF16)],
        compiler_params=_cparams(("parallel", "arbitrary")),
    )(x, g, sc, sh, w)


GLA_LEVELS = (32, 16, 8, 4, 2, 1)


def _gla_kernel(q_ref, k_ref, v_ref, r_ref, a_ref, wa_ref, ba_ref, g_ref, o_ref,
                state_ref, *, n_chunks):
    C = GLA_CHUNK

    @pl.when(pl.program_id(1) == 0)
    def _():
        state_ref[...] = jnp.zeros(state_ref.shape, F32)

    nl = len(GLA_LEVELS) + 1
    ri = lax.broadcasted_iota(jnp.int32, (C, C), 0)
    ci = lax.broadcasted_iota(jnp.int32, (C, C), 1)
    pieces = [ci <= ri]
    masks = [ci == ri]
    for s in GLA_LEVELS:
        blk_r = ri // (2 * s)
        bound = blk_r * (2 * s) + (s - 1)
        pieces.append(ci <= bound)
        masks.append((blk_r == ci // (2 * s)) & (ri % (2 * s) >= s) & (ci % (2 * s) < s))
    prefix = jnp.concatenate([jnp.where(p, 1.0, 0.0).astype(BF16) for p in pieces], axis=0)

    wa = wa_ref[...].astype(BF16)
    ba = ba_ref[...]
    gain = g_ref[...]
    scale = GLA_DK ** -0.5

    def chunk(c, carry):
        off = pl.multiple_of(c * C, C)
        rows = pl.ds(off, C)
        for b in range(q_ref.shape[0]):
            z = _dot(a_ref[b, rows, :].astype(BF16), wa) + ba
            la = -(jnp.maximum(-z, 0.0) + jnp.log1p(jnp.exp(-jnp.abs(z)))) * (1.0 / 16.0)
            hi, mid, lo = _split3(la)
            pref_all = _dot(prefix, hi) + _dot(prefix, mid) + _dot(prefix, lo)

            for h in range(GLA_HEADS):
                kcols = slice(h * GLA_DK, (h + 1) * GLA_DK)
                vcols = slice(h * GLA_DV, (h + 1) * GLA_DV)
                pref = pref_all[:, kcols]
                q = q_ref[b, rows, kcols] * scale
                k = k_ref[b, rows, kcols]
                v = v_ref[b, rows, vcols].astype(BF16)
                G = pref[0:C]
                g_last = G[C - 1:C, :]

                attn = jnp.where(masks[0], _dot_nt(q.astype(BF16), k.astype(BF16)), 0.0)
                for l in range(1, nl):
                    e = jnp.exp(-jnp.abs(G - pref[l * C:(l + 1) * C]))
                    a_l = _dot_nt((q * e).astype(BF16), (k * e).astype(BF16))
                    attn = attn + jnp.where(masks[l], a_l, 0.0)

                st = state_ref[b, h]
                o = (_dot(attn.astype(BF16), v)
                     + _dot_nt((q * jnp.exp(G)).astype(BF16), st.astype(BF16)))
                k_dec = (k * jnp.exp(g_last - G)).astype(BF16)
                state_ref[b, h] = st * jnp.exp(g_last) + _dot_tn(v, k_dec)

                y = o * lax.rsqrt(jnp.mean(o * o, axis=-1, keepdims=True) + EPS) * gain[:, vcols]
                r = r_ref[b, rows, vcols]
                o_ref[b, rows, vcols] = (y * (r / (1.0 + jnp.exp(-r)))).astype(o_ref.dtype)
        return carry

    lax.fori_loop(0, n_chunks, chunk, 0, unroll=2)


def _gla(proj, wa2p, ba, norm_g, batch, seq):
    lc = min(512, seq)
    nb = 2 if batch % 2 == 0 else 1
    blk = lambda width, col: pl.BlockSpec((nb, lc, width), lambda b, s: (b, s, col))
    return pl.pallas_call(
        functools.partial(_gla_kernel, n_chunks=lc // GLA_CHUNK),
        name="gla",
        grid=(batch // nb, seq // lc),
        in_specs=[
            blk(GLA_QK, 0),
            blk(GLA_QK, 1),
            blk(GLA_V, 1),
            blk(GLA_V, 2),
            blk(LANES, COL_A),
            pl.BlockSpec((LANES, GLA_QK), lambda b, s: (0, 0)),
            pl.BlockSpec((1, GLA_QK), lambda b, s: (0, 0)),
            pl.BlockSpec((1, GLA_V), lambda b, s: (0, 0)),
        ],
        out_specs=blk(GLA_V, 0),
        out_shape=jax.ShapeDtypeStruct((batch, seq, GLA_V), BF16),
        scratch_shapes=[pltpu.VMEM((nb, GLA_HEADS, GLA_DV, GLA_DK), F32)],
        compiler_params=_cparams(("parallel", "arbitrary")),
    )(proj, proj, proj, proj, proj, wa2p, ba, norm_g)


def _gmlp_chunk(u_ref, v_ref, ws_ref, bias_ref, vg_ref, og_ref, ym_scr, rows):
    for h in range(GMLP_HEADS):
        cols = slice(h * GMLP_DH, (h + 1) * GMLP_DH)
        v = _gelu(v_ref[rows, cols])
        v = v * lax.rsqrt(jnp.mean(v * v, axis=-1, keepdims=True) + EPS) * vg_ref[:, cols]
        sv = _dot(ws_ref[h], v.astype(BF16)) + bias_ref[:, cols]
        y = _gelu(u_ref[rows, cols]) * sv
        y = y * lax.rsqrt(jnp.mean(y * y, axis=-1, keepdims=True) + EPS) * og_ref[:, cols]
        ym_scr[rows, cols] = y.astype(ym_scr.dtype)


def _outproj_kernel(yg_ref, u_ref, v_ref, ws_ref, bias_ref, vg_ref, og_ref, wg_ref, wm_ref,
                    x_ref, gt_ref, g2_ref, sc_ref, sh_ref, x1_ref, h2t_ref, ym_scr):
    mix = _dot(yg_ref[...], wg_ref[...])
    for c in range(yg_ref.shape[0] // GMLP_CHUNK):
        _gmlp_chunk(u_ref, v_ref, ws_ref, bias_ref, vg_ref, og_ref, ym_scr,
                    slice(c * GMLP_CHUNK, (c + 1) * GMLP_CHUNK))
    mix = mix + _dot(ym_scr[...], wm_ref[...])
    x1 = x_ref[...] + gt_ref[...] * mix
    x1_ref[...] = x1
    y = x1 * lax.rsqrt(jnp.mean(x1 * x1, axis=-1, keepdims=True) + EPS) * g2_ref[...]
    h2 = y * (1.0 + sc_ref[...]) + sh_ref[...]
    h2t_ref[...] = h2.T.astype(BF16)


def _peer_tm(t):
    return min(512, t)


def _outproj(yg, proj, ws_causal, bias_full, vnorm_g, out_g, w_out, x, gt1, g2, sc2, sh2, seq):
    t, d = x.shape
    tm = min(256, seq)
    per_b = seq // tm
    half = d // 2
    ptm = _peer_tm(t)
    per_p = ptm // tm
    c = GMLP_CHUNK
    bvec = pl.BlockSpec((None, 1, d), lambda i: (i // per_b, 0, 0))
    return pl.pallas_call(
        _outproj_kernel,
        name="outproj",
        grid=(t // tm,),
        in_specs=[
            pl.BlockSpec((tm, half), lambda i: (i, 0)),
            pl.BlockSpec((tm, GMLP_WIDTH), lambda i: (i, COL_U)),
            pl.BlockSpec((tm, GMLP_WIDTH), lambda i: (i, COL_VSP)),
            pl.BlockSpec((GMLP_HEADS, c, c), lambda i: (0, 0, 0)),
            pl.BlockSpec((c, GMLP_WIDTH), lambda i: (0, 0)),
            pl.BlockSpec((1, GMLP_WIDTH), lambda i: (0, 0)),
            pl.BlockSpec((1, GMLP_WIDTH), lambda i: (0, 0)),
            pl.BlockSpec((half, d), lambda i: (0, 0)),
            pl.BlockSpec((half, d), lambda i: (1, 0)),
            pl.BlockSpec((tm, d), lambda i: (i, 0)),
            bvec,
            pl.BlockSpec((1, d), lambda i: (0, 0)),
            bvec,
            bvec,
        ],
        out_specs=[pl.BlockSpec((tm, d), lambda i: (i, 0)),
                   pl.BlockSpec((None, d, tm), lambda i: (i // per_p, 0, i % per_p))],
        out_shape=[jax.ShapeDtypeStruct((t, d), F32),
                   jax.ShapeDtypeStruct((t // ptm, d, ptm), BF16)],
        scratch_shapes=[pltpu.VMEM((tm, GMLP_WIDTH), BF16)],
        compiler_params=_cparams(("parallel",)),
    )(yg, proj, proj, ws_causal, bias_full, vnorm_g, out_g, w_out, w_out, x, gt1, g2, sc2, sh2)


CAND_COLS = tuple(PEER_TOPK // (r + 1) for r in range(PEER_TOPK))
BIG_IDX = float(1 << 20)


def _top16_exact(s, vals_ref):
    key = lax.broadcasted_iota(jnp.int32, s.shape, 0).astype(F32)
    rank = jnp.full(s.shape, float(PEER_TOPK), F32)
    for r in range(PEER_TOPK):
        m = jnp.max(s, axis=0, keepdims=True)
        first = jnp.min(jnp.where(s == m, key, BIG_IDX), axis=0, keepdims=True)
        sel = key == first
        rank = jnp.where(sel, float(r), rank)
        s = jnp.where(sel, NEG_INF, s)
        vals_ref[r:r + 1, :] = m
    return rank


def _batcher_pairs(n):
    def merge(lo, hi, r):
        step = r * 2
        if step < hi - lo:
            yield from merge(lo, hi, step)
            yield from merge(lo + r, hi, step)
            yield from ((i, i + r) for i in range(lo + r, hi - r, step))
        else:
            yield (lo, lo + r)

    def sort(lo, hi):
        if hi - lo >= 1:
            mid = lo + (hi - lo) // 2
            yield from sort(lo, mid)
            yield from sort(mid + 1, hi)
            yield from merge(lo, hi, 1)

    return tuple(sort(0, n - 1))


SORT16_PAIRS = _batcher_pairs(PEER_TOPK)


def _top16_sorted(s, vals_ref):
    n = PEER_TOPK
    v = [s[i * SUBLANES:(i + 1) * SUBLANES, :] for i in range(n)]
    for i, j in SORT16_PAIRS:
        v[i], v[j] = jnp.maximum(v[i], v[j]), jnp.minimum(v[i], v[j])
    for shift in (4, 2, 1):
        v = [jnp.maximum(v[i], pltpu.roll(v[n - 1 - i], SUBLANES - shift, axis=0))
             for i in range(n)]
        for dist in (8, 4, 2, 1):
            for i in range(n):
                if i % (2 * dist) < dist:
                    j = i + dist
                    v[i], v[j] = jnp.maximum(v[i], v[j]), jnp.minimum(v[i], v[j])
    rows = [x[0:1, :] for x in v]
    for r in range(n):
        vals_ref[r:r + 1, :] = rows[r]
    count = jnp.sum(jnp.where(s >= rows[n - 1], 1.0, 0.0), axis=0, keepdims=True)
    for r in range(n - 1):
        count = count + jnp.where(rows[r] == rows[r + 1], 1.0, 0.0)
    return count


def _route_tile(s1, s2, v1_scr, v2_scr, ties):
    width = s1.shape[1]
    sub = lax.broadcasted_iota(jnp.int32, (SUBLANES, width), 0).astype(F32)
    if ties:
        rank1 = _top16_exact(s1, v1_scr)
        rank2 = _top16_exact(s2, v2_scr)
        marked1 = marked2 = jnp.full((1, width), float(PEER_TOPK), F32)
    else:
        marked1 = _top16_sorted(s1, v1_scr)
        marked2 = _top16_sorted(s2, v2_scr)
        rank2 = jnp.zeros(s2.shape, F32)
        for r in range(PEER_TOPK):
            rank2 = jnp.where(v2_scr[r:r + 1, :] > s2, float(r + 1), rank2)

    v2a = v2_scr[0:8, :]
    v2b = v2_scr[8:16, :]
    cands = [v1_scr[0:1, :] + v2a, v1_scr[0:1, :] + v2b]
    ids = [sub, sub + 8]
    for r in range(1, 8):
        cands.append(jnp.where(sub < CAND_COLS[r], v1_scr[r:r + 1, :] + v2a, NEG_INF))
        ids.append(sub + PEER_TOPK * r)
    cands.append(v1_scr[8:16, :] + v2_scr[0:1, :])
    ids.append((sub + 8) * PEER_TOPK)
    taken = [jnp.zeros((SUBLANES, width), F32) for _ in cands]
    mx = v1_scr[0:1, :] + v2_scr[0:1, :]
    zsum = jnp.zeros((1, width), F32)
    for _ in range(PEER_TOPK):
        m = functools.reduce(jnp.maximum, cands)
        m = jnp.max(m, axis=0, keepdims=True)
        sels = [c == m for c in cands]
        if ties:
            first = functools.reduce(
                jnp.minimum, [jnp.where(s, i, BIG_IDX) for s, i in zip(sels, ids)])
            first = jnp.min(first, axis=0, keepdims=True)
            sels = [i == first for i in ids]
        cands = [jnp.where(s, NEG_INF, c) for s, c in zip(sels, cands)]
        taken = [jnp.where(s, 1.0, t) for s, t in zip(sels, taken)]
        zsum = zsum + jnp.exp(m - mx)

    counts = [jnp.sum(taken[0] + taken[1], axis=0, keepdims=True)]
    for r in range(1, 8):
        counts.append(jnp.sum(taken[r + 1], axis=0, keepdims=True))
    for r in range(8, PEER_TOPK):
        counts.append(taken[9][r - 8:r - 7, :])
    n1 = jnp.zeros(s1.shape, F32)
    for r in range(PEER_TOPK):
        hit = (rank1 == r) if ties else (s1 == v1_scr[r:r + 1, :])
        n1 = jnp.where(hit, counts[r], n1)

    p1 = jnp.exp(s1 - v1_scr[0:1, :]) * (1.0 / zsum)
    p2 = jnp.exp(s2 - v2_scr[0:1, :])
    return n1, p1, rank2, p2, [marked1, marked2, functools.reduce(jnp.add, counts)]


def _route_kernel(h_ref, wq_ref, k1_ref, k2_ref, rank2_ref, p2_ref, n1_ref, p1_ref,
                  s1_scr, s2_scr, v1_scr, v2_scr):
    half = PEER_DQ // 2

    def scores(h):
        qt = _dot(wq_ref[h], h_ref[...])
        s1_scr[h] = _dot(k1_ref[h], qt[:half].astype(BF16))
        s2_scr[h] = _dot(k2_ref[h], qt[half:].astype(BF16))

    scores(0)

    def body(h, carry):
        s1 = s1_scr[h]
        s2 = s2_scr[h]
        scores(jnp.minimum(h + 1, PEER_HEADS - 1))

        def run(ties):
            n1, p1, rank2, p2, marked = _route_tile(s1, s2, v1_scr, v2_scr, ties)
            n1_ref[h] = n1
            p1_ref[h] = p1
            rank2_ref[h] = rank2.astype(rank2_ref.dtype)
            p2_ref[h] = p2.astype(p2_ref.dtype)
            return marked

        marked = run(ties=False)
        bad = functools.reduce(
            jnp.maximum, [jnp.where(mk == PEER_TOPK, 0.0, 1.0) for mk in marked])

        @pl.when(jnp.max(bad) > 0.0)
        def _():
            run(ties=True)

        return carry

    lax.fori_loop(0, PEER_HEADS, body, 0)


def _route(h2t, wq, k1, k2):
    n_t, d, tm = h2t.shape
    t = n_t * tm
    hk = (PEER_HEADS, N_KEYS, tm)
    tiled = (t // tm,) + hk
    out_spec = pl.BlockSpec((None,) + hk, lambda i: (i, 0, 0, 0))
    kspec = pl.BlockSpec((PEER_HEADS, N_KEYS, PEER_DQ // 2), lambda i: (0, 0, 0))
    wq_heads = wq.reshape(d, PEER_HEADS, PEER_DQ).transpose(1, 2, 0)
    return pl.pallas_call(
        _route_kernel,
        name="peer_route",
        grid=(t // tm,),
        in_specs=[
            pl.BlockSpec((None, d, tm), lambda i: (i, 0, 0)),
            pl.BlockSpec(wq_heads.shape, lambda i: (0, 0, 0)),
            kspec, kspec,
        ],
        out_specs=[out_spec, out_spec, out_spec, out_spec],
        out_shape=[
            jax.ShapeDtypeStruct(tiled, BF16),
            jax.ShapeDtypeStruct(tiled, BF16),
            jax.ShapeDtypeStruct(tiled, F32),
            jax.ShapeDtypeStruct(tiled, F32),
        ],
        scratch_shapes=[
            pltpu.VMEM(hk, F32), pltpu.VMEM(hk, F32),
            pltpu.VMEM((PEER_TOPK, tm), F32), pltpu.VMEM((PEER_TOPK, tm), F32),
        ],
        compiler_params=_cparams(("parallel",)),
    )(h2t, wq_heads, k1, k2)


def _peer_kernel(ht_ref, pu_ref, pvt_ref, rank2_ref, p2_ref, n1_ref, p1_ref, o_ref,
                 pa_scr, pb_scr, *, a_rows, n_e):
    j = pl.program_id(1)
    group = 1
    acc_split = 1
    n_groups = a_rows // group
    o_rows = o_ref.shape[0] // n_groups

    def activations(k, p_scr):
        base = pl.multiple_of(k * (group * N_KEYS), group * N_KEYS)
        scores = _dot(pu_ref[pl.ds(base, group * N_KEYS), :], ht_ref[...])
        for g in range(group):
            a = j * a_rows + k * group + g
            w = None
            for h in range(PEER_HEADS):
                n1 = n1_ref[h, pl.ds(a, 1), :].astype(BF16)
                p1 = p1_ref[h, pl.ds(a, 1), :].astype(BF16)
                term = jnp.where(rank2_ref[h] < n1, p2_ref[h], jnp.zeros((), BF16)) * p1
                w = term if w is None else w + term
            act = _gelu(scores[g * N_KEYS:(g + 1) * N_KEYS, :])
            p_scr[pl.ds(base + g * N_KEYS, N_KEYS), :] = (act * w.astype(F32)).astype(BF16)

    def accumulate(k, p_scr):
        piece = o_rows // acc_split
        for s in range(acc_split):
            rows = pl.ds(pl.multiple_of(k * o_rows + s * piece, piece), piece)
            o_ref[rows, :] += _dot(pvt_ref[rows, :], p_scr[...])

    def both(p_new, p_old):
        def body(k, carry):
            activations(k, p_new)
            accumulate(k, p_old)
            return carry
        lax.fori_loop(0, n_groups, body, 0, unroll=n_groups)

    @pl.when(j == 0)
    def _():
        o_ref[...] = jnp.zeros(o_ref.shape, F32)
        lax.fori_loop(0, n_groups, lambda k, c: (activations(k, pa_scr), c)[1], 0)

    @pl.when((j > 0) & (j < n_e) & (j % 2 == 0))
    def _():
        both(pa_scr, pb_scr)

    @pl.when((j < n_e) & (j % 2 == 1))
    def _():
        both(pb_scr, pa_scr)

    @pl.when(j == n_e)
    def _():
        p_last = pb_scr if n_e % 2 == 0 else pa_scr
        lax.fori_loop(0, n_groups, lambda k, c: (accumulate(k, p_last), c)[1], 0)


PEER_TE = 512


def _peer_dense(ht, pu, pvt, rank2, p2, n1, p1):
    n_t, d, tm = ht.shape
    te = PEER_TE
    n_e = N_EXPERTS // te
    rspec = pl.BlockSpec((None, PEER_HEADS, N_KEYS, tm), lambda i, j: (i, 0, 0, 0))
    return pl.pallas_call(
        functools.partial(_peer_kernel, a_rows=te // N_KEYS, n_e=n_e),
        name="peer_dense",
        grid=(n_t, n_e + 1),
        in_specs=[
            pl.BlockSpec((None, d, tm), lambda i, j: (i, 0, 0)),
            pl.BlockSpec((te, d), lambda i, j: (jnp.minimum(j, n_e - 1), 0)),
            pl.BlockSpec((None, d, te), lambda i, j: (jnp.maximum(j - 1, 0), 0, 0)),
            rspec, rspec, rspec, rspec,
        ],
        out_specs=pl.BlockSpec((None, d, tm), lambda i, j: (i, 0, 0)),
        out_shape=jax.ShapeDtypeStruct((n_t, d, tm), F32),
        scratch_shapes=[pltpu.VMEM((te, tm), BF16), pltpu.VMEM((te, tm), BF16)],
        compiler_params=_cparams(("parallel", "arbitrary")),
    )(ht, pu, pvt, rank2, p2, n1, p1)


def _resid_kernel(x_ref, yt_ref, gt_ref, g_ref, o_ref, *, final):
    x2 = x_ref[...] + gt_ref[...] * yt_ref[...].T
    if final:
        x2 = x2 * lax.rsqrt(jnp.mean(x2 * x2, axis=-1, keepdims=True) + EPS) * g_ref[...]
    o_ref[...] = x2


def _resid(x1, yt, gt2, final_g, seq, final):
    t, d = x1.shape
    tm = min(256, seq)
    per_b = seq // tm
    per_p = yt.shape[2] // tm
    return pl.pallas_call(
        functools.partial(_resid_kernel, final=final),
        name="peer_resid",
        grid=(t // tm,),
        in_specs=[
            pl.BlockSpec((tm, d), lambda i: (i, 0)),
            pl.BlockSpec((None, d, tm), lambda i: (i // per_p, 0, i % per_p)),
            pl.BlockSpec((None, 1, d), lambda i: (i // per_b, 0, 0)),
            pl.BlockSpec((1, d), lambda i: (0, 0)),
        ],
        out_specs=pl.BlockSpec((tm, d), lambda i: (i, 0)),
        out_shape=jax.ShapeDtypeStruct((t, d), F32),
        compiler_params=_cparams(("parallel",)),
    )(x1, yt, gt2, final_g)


def _layout_w_in(w_in_l):
    d = w_in_l.shape[0]
    o_a = 2 * GLA_QK + 2 * GLA_V
    o_u = o_a + GLA_RANK
    pad = jnp.zeros((d, PROJ_COLS - (w_in_l.shape[1] - GLA_RANK) - GLA_RANK), w_in_l.dtype)
    w = jnp.concatenate([w_in_l[:, :o_a], w_in_l[:, o_u:], w_in_l[:, o_a:o_u], pad], axis=1)
    w = w.astype(BF16).reshape(d, PROJ_COLS // PROJ_TN, PROJ_TN)
    return w.transpose(1, 0, 2)


def kernel(x, c, ada_w, ada_b, norm1_g, w_in, gla_w_a2, gla_b_a, gla_norm_g, gmlp_vnorm_g,
           gmlp_ws, gmlp_b, gmlp_out_g, w_out, norm2_g, peer_wq, peer_k1, peer_k2, peer_u,
           peer_v, final_g):
    batch, seq, d = x.shape
    depth = ada_w.shape[0]
    t = batch * seq
    xf = x.reshape(t, d)

    mod = _modulation(c, ada_w, ada_b)
    causal = jnp.tril(jnp.ones((GMLP_CHUNK, GMLP_CHUNK), F32))

    for l in range(depth):
        sh1, sc1, gt1, sh2, sc2, gt2 = [m.reshape(batch, 1, d) for m in jnp.split(mod[l], 6, axis=-1)]

        proj = _inproj(xf, norm1_g[l].reshape(1, d), sc1, sh1, _layout_w_in(w_in[l]), seq)

        wa2p = jnp.zeros((LANES, GLA_QK), F32).at[:GLA_RANK].set(gla_w_a2[l])
        y_gla = _gla(proj.reshape(batch, seq, PROJ_COLS), wa2p, gla_b_a[l].reshape(1, GLA_QK),
                     gla_norm_g[l].reshape(1, GLA_V), batch, seq).reshape(t, GLA_V)

        bias_full = jnp.repeat(gmlp_b[l].T, GMLP_DH, axis=1)
        x1, h2t = _outproj(
            y_gla, proj, (gmlp_ws[l] * causal).astype(BF16), bias_full,
            gmlp_vnorm_g[l].reshape(1, GMLP_WIDTH), gmlp_out_g[l].reshape(1, GMLP_WIDTH),
            w_out[l].astype(BF16), xf, gt1, norm2_g[l].reshape(1, d), sc2, sh2, seq)

        rank2, p2, n1, p1 = _route(h2t, peer_wq[l].astype(BF16), peer_k1[l].astype(BF16),
                                   peer_k2[l].astype(BF16))
        pvt = peer_v[l].reshape(N_EXPERTS // PEER_TE, PEER_TE, d).transpose(0, 2, 1).astype(BF16)
        yt = _peer_dense(h2t, peer_u[l].astype(BF16), pvt, rank2, p2, n1, p1)
        xf = _resid(x1, yt, gt2, final_g.reshape(1, d), seq, final=(l == depth - 1))

    return xf.reshape(batch, seq, d)
```

```python
import functools

import jax
import jax.numpy as jnp
from jax import lax
from jax.experimental import pallas as pl
from jax.experimental.pallas import tpu as pltpu

F32 = jnp.float32
BF16 = jnp.bfloat16

D_MODEL = 2048
EPS = 1e-6
GLA_HEADS = 4
GLA_DV = 256
GLA_DK = 128
GLA_QK = GLA_HEADS * GLA_DK
GLA_V = GLA_HEADS * GLA_DV
GLA_RANK = 16
GLA_CHUNK = 64
GMLP_HEADS = 8
GMLP_WIDTH = 1024
GMLP_DH = 128
GMLP_CHUNK = 128
N_KEYS = 128
N_EXPERTS = N_KEYS * N_KEYS
PEER_HEADS = 8
PEER_TOPK = 16
PEER_DQ = 256

LANES = 128
SUBLANES = 8
PROJ_COLS = 5376
COL_U = 3
COL_VSP = 4
COL_A = 40
VMEM_LIMIT = 56 * 1024 * 1024

NEG_INF = float("-inf")


def _cparams(sem):
    return pltpu.CompilerParams(dimension_semantics=sem, vmem_limit_bytes=VMEM_LIMIT)


def _gelu(x):
    c = 0.7978845608028654
    return 0.5 * x * (1.0 + jnp.tanh(c * (x + 0.044715 * (x * x * x))))


def _split3(x):
    hi = x.astype(BF16)
    r1 = x - hi.astype(F32)
    mid = r1.astype(BF16)
    lo = (r1 - mid.astype(F32)).astype(BF16)
    return hi, mid, lo


def _dot(a, b):
    return jnp.dot(a, b, preferred_element_type=F32)


def _dot_nt(a, b):
    return lax.dot_general(a, b, (((1,), (1,)), ((), ())), preferred_element_type=F32)


def _dot_tn(a, b):
    return lax.dot_general(a, b, (((0,), (0,)), ((), ())), preferred_element_type=F32)


def _mod_kernel(cb_ref, w_ref, b_ref, o_ref):
    k = pl.program_id(1)
    n = w_ref.shape[1]
    c = cb_ref[...]
    cond = c / (1.0 + jnp.exp(-c))

    @pl.when(k == 0)
    def _():
        o_ref[...] = jnp.broadcast_to(b_ref[...], o_ref.shape)

    for j in range(n // LANES):
        cols = slice(j * LANES, (j + 1) * LANES)
        w = w_ref[:, cols]
        for b in range(cb_ref.shape[0]):
            o_ref[b:b + 1, cols] += jnp.sum(w * cond[b], axis=0, keepdims=True)


def _modulation(c, ada_w, ada_b):
    depth, d, n = ada_w.shape
    b = c.shape[0]
    tk = 128
    cb = jnp.broadcast_to(c[:, :, None], (b, d, LANES))
    return pl.pallas_call(
        _mod_kernel,
        name="adaln_mod",
        grid=(depth, d // tk),
        in_specs=[
            pl.BlockSpec((b, tk, LANES), lambda l, k: (0, k, 0)),
            pl.BlockSpec((None, tk, n), lambda l, k: (l, k, 0)),
            pl.BlockSpec((None, 1, n), lambda l, k: (l, 0, 0)),
        ],
        out_specs=pl.BlockSpec((None, b, n), lambda l, k: (l, 0, 0)),
        out_shape=jax.ShapeDtypeStruct((depth, b, n), F32),
        compiler_params=_cparams(("parallel", "arbitrary")),
    )(cb, ada_w, ada_b.reshape(depth, 1, n))


def _inproj_kernel(x_ref, g_ref, sc_ref, sh_ref, w_ref, o_ref, h_scr):
    @pl.when(pl.program_id(1) == 0)
    def _():
        x = x_ref[...]
        ms = jnp.mean(x * x, axis=-1, keepdims=True)
        y = x * lax.rsqrt(ms + EPS) * g_ref[...]
        h_scr[...] = (y * (1.0 + sc_ref[...]) + sh_ref[...]).astype(BF16)

    o_ref[...] = _dot(h_scr[...], w_ref[pl.program_id(1)])


PROJ_TN = 1792


def _inproj(x, g, sc, sh, w, seq):
    t, d = x.shape
    n_tiles, _, tn = w.shape
    n = n_tiles * tn
    tm = min(512, seq)
    per_b = seq // tm
    return pl.pallas_call(
        _inproj_kernel,
        name="norm_inproj",
        grid=(t // tm, n_tiles),
        in_specs=[
            pl.BlockSpec((tm, d), lambda i, j: (i, 0)),
            pl.BlockSpec((1, d), lambda i, j: (0, 0)),
            pl.BlockSpec((None, 1, d), lambda i, j: (i // per_b, 0, 0)),
            pl.BlockSpec((None, 1, d), lambda i, j: (i // per_b, 0, 0)),
            pl.BlockSpec(w.shape, lambda i, j: (0, 0, 0), pipeline_mode=pl.Buffered(1)),
        ],
        out_specs=pl.BlockSpec((tm, tn), lambda i, j: (i, j)),
        out_shape=jax.ShapeDtypeStruct((t, n), F32),
        scratch_shapes=[pltpu.VMEM((tm, d), BF16)],
        compiler_params=_cparams(("parallel", "arbitrary")),
    )(x, g, sc, sh, w)


GLA_LEVELS = (32, 16, 8, 4, 2, 1)


def _gla_kernel(q_ref, k_ref, v_ref, r_ref, a_ref, wa_ref, ba_ref, g_ref, o_ref,
                state_ref, *, n_chunks):
    C = GLA_CHUNK

    @pl.when(pl.program_id(1) == 0)
    def _():
        state_ref[...] = jnp.zeros(state_ref.shape, F32)

    nl = len(GLA_LEVELS) + 1
    ri = lax.broadcasted_iota(jnp.int32, (C, C), 0)
    ci = lax.broadcasted_iota(jnp.int32, (C, C), 1)
    pieces = [ci <= ri]
    masks = [ci == ri]
    for s in GLA_LEVELS:
        blk_r = ri // (2 * s)
        bound = blk_r * (2 * s) + (s - 1)
        pieces.append(ci <= bound)
        masks.append((blk_r == ci // (2 * s)) & (ri % (2 * s) >= s) & (ci % (2 * s) < s))
    prefix = jnp.concatenate([jnp.where(p, 1.0, 0.0).astype(BF16) for p in pieces], axis=0)

    wa = wa_ref[...].astype(BF16)
    ba = ba_ref[...]
    gain = g_ref[...]
    scale = GLA_DK ** -0.5

    def chunk(c, carry):
        off = pl.multiple_of(c * C, C)
        rows = pl.ds(off, C)
        for b in range(q_ref.shape[0]):
            z = _dot(a_ref[b, rows, :].astype(BF16), wa) + ba
            la = -(jnp.maximum(-z, 0.0) + jnp.log1p(jnp.exp(-jnp.abs(z)))) * (1.0 / 16.0)
            hi, mid, lo = _split3(la)
            pref_all = _dot(prefix, hi) + _dot(prefix, mid) + _dot(prefix, lo)

            for h in range(GLA_HEADS):
                kcols = slice(h * GLA_DK, (h + 1) * GLA_DK)
                vcols = slice(h * GLA_DV, (h + 1) * GLA_DV)
                pref = pref_all[:, kcols]
                q = q_ref[b, rows, kcols] * scale
                k = k_ref[b, rows, kcols]
                v = v_ref[b, rows, vcols].astype(BF16)
                G = pref[0:C]
                g_last = G[C - 1:C, :]

                attn = jnp.where(masks[0], _dot_nt(q.astype(BF16), k.astype(BF16)), 0.0)
                for l in range(1, nl):
                    e = jnp.exp(-jnp.abs(G - pref[l * C:(l + 1) * C]))
                    a_l = _dot_nt((q * e).astype(BF16), (k * e).astype(BF16))
                    attn = attn + jnp.where(masks[l], a_l, 0.0)

                st = state_ref[b, h]
                o = (_dot(attn.astype(BF16), v)
                     + _dot_nt((q * jnp.exp(G)).astype(BF16), st.astype(BF16)))
                k_dec = (k * jnp.exp(g_last - G)).astype(BF16)
                state_ref[b, h] = st * jnp.exp(g_last) + _dot_tn(v, k_dec)

                y = o * lax.rsqrt(jnp.mean(o * o, axis=-1, keepdims=True) + EPS) * gain[:, vcols]
                r = r_ref[b, rows, vcols]
                o_ref[b, rows, vcols] = (y * (r / (1.0 + jnp.exp(-r)))).astype(o_ref.dtype)
        return carry

    lax.fori_loop(0, n_chunks, chunk, 0, unroll=2)


def _gla(proj, wa2p, ba, norm_g, batch, seq):
    lc = min(512, seq)
    nb = 2 if batch % 2 == 0 else 1
    blk = lambda width, col: pl.BlockSpec((nb, lc, width), lambda b, s: (b, s, col))
    return pl.pallas_call(
        functools.partial(_gla_kernel, n_chunks=lc // GLA_CHUNK),
        name="gla",
        grid=(batch // nb, seq // lc),
        in_specs=[
            blk(GLA_QK, 0),
            blk(GLA_QK, 1),
            blk(GLA_V, 1),
            blk(GLA_V, 2),
            blk(LANES, COL_A),
            pl.BlockSpec((LANES, GLA_QK), lambda b, s: (0, 0)),
            pl.BlockSpec((1, GLA_QK), lambda b, s: (0, 0)),
            pl.BlockSpec((1, GLA_V), lambda b, s: (0, 0)),
        ],
        out_specs=blk(GLA_V, 0),
        out_shape=jax.ShapeDtypeStruct((batch, seq, GLA_V), BF16),
        scratch_shapes=[pltpu.VMEM((nb, GLA_HEADS, GLA_DV, GLA_DK), F32)],
        compiler_params=_cparams(("parallel", "arbitrary")),
    )(proj, proj, proj, proj, proj, wa2p, ba, norm_g)


def _gmlp_chunk(u_ref, v_ref, ws_ref, bias_ref, vg_ref, og_ref, ym_scr, rows):
    for h in range(GMLP_HEADS):
        cols = slice(h * GMLP_DH, (h + 1) * GMLP_DH)
        v = _gelu(v_ref[rows, cols])
        v = v * lax.rsqrt(jnp.mean(v * v, axis=-1, keepdims=True) + EPS) * vg_ref[:, cols]
        sv = _dot(ws_ref[h], v.astype(BF16)) + bias_ref[:, cols]
        y = _gelu(u_ref[rows, cols]) * sv
        y = y * lax.rsqrt(jnp.mean(y * y, axis=-1, keepdims=True) + EPS) * og_ref[:, cols]
        ym_scr[rows, cols] = y.astype(ym_scr.dtype)


def _outproj_kernel(yg_ref, u_ref, v_ref, ws_ref, bias_ref, vg_ref, og_ref, wg_ref, wm_ref,
                    x_ref, gt_ref, g2_ref, sc_ref, sh_ref, x1_ref, h2t_ref, ym_scr):
    mix = _dot(yg_ref[...], wg_ref[...])
    for c in range(yg_ref.shape[0] // GMLP_CHUNK):
        _gmlp_chunk(u_ref, v_ref, ws_ref, bias_ref, vg_ref, og_ref, ym_scr,
                    slice(c * GMLP_CHUNK, (c + 1) * GMLP_CHUNK))
    mix = mix + _dot(ym_scr[...], wm_ref[...])
    x1 = x_ref[...] + gt_ref[...] * mix
    x1_ref[...] = x1
    y = x1 * lax.rsqrt(jnp.mean(x1 * x1, axis=-1, keepdims=True) + EPS) * g2_ref[...]
    h2 = y * (1.0 + sc_ref[...]) + sh_ref[...]
    h2t_ref[...] = h2.T.astype(BF16)


def _peer_tm(t):
    return min(512, t)


def _outproj(yg, proj, ws_causal, bias_full, vnorm_g, out_g, w_out, x, gt1, g2, sc2, sh2, seq):
    t, d = x.shape
    tm = min(256, seq)
    per_b = seq // tm
    half = d // 2
    ptm = _peer_tm(t)
    per_p = ptm // tm
    c = GMLP_CHUNK
    bvec = pl.BlockSpec((None, 1, d), lambda i: (i // per_b, 0, 0))
    return pl.pallas_call(
        _outproj_kernel,
        name="outproj",
        grid=(t // tm,),
        in_specs=[
            pl.BlockSpec((tm, half), lambda i: (i, 0)),
            pl.BlockSpec((tm, GMLP_WIDTH), lambda i: (i, COL_U)),
            pl.BlockSpec((tm, GMLP_WIDTH), lambda i: (i, COL_VSP)),
            pl.BlockSpec((GMLP_HEADS, c, c), lambda i: (0, 0, 0)),
            pl.BlockSpec((c, GMLP_WIDTH), lambda i: (0, 0)),
            pl.BlockSpec((1, GMLP_WIDTH), lambda i: (0, 0)),
            pl.BlockSpec((1, GMLP_WIDTH), lambda i: (0, 0)),
            pl.BlockSpec((half, d), lambda i: (0, 0)),
            pl.BlockSpec((half, d), lambda i: (1, 0)),
            pl.BlockSpec((tm, d), lambda i: (i, 0)),
            bvec,
            pl.BlockSpec((1, d), lambda i: (0, 0)),
            bvec,
            bvec,
        ],
        out_specs=[pl.BlockSpec((tm, d), lambda i: (i, 0)),
                   pl.BlockSpec((None, d, tm), lambda i: (i // per_p, 0, i % per_p))],
        out_shape=[jax.ShapeDtypeStruct((t, d), F32),
                   jax.ShapeDtypeStruct((t // ptm, d, ptm), BF16)],
        scratch_shapes=[pltpu.VMEM((tm, GMLP_WIDTH), BF16)],
        compiler_params=_cparams(("parallel",)),
    )(yg, proj, proj, ws_causal, bias_full, vnorm_g, out_g, w_out, w_out, x, gt1, g2, sc2, sh2)


CAND_COLS = tuple(PEER_TOPK // (r + 1) for r in range(PEER_TOPK))
BIG_IDX = float(1 << 20)


def _top16_exact(s, vals_ref):
    key = lax.broadcasted_iota(jnp.int32, s.shape, 0).astype(F32)
    rank = jnp.full(s.shape, float(PEER_TOPK), F32)
    for r in range(PEER_TOPK):
        m = jnp.max(s, axis=0, keepdims=True)
        first = jnp.min(jnp.where(s == m, key, BIG_IDX), axis=0, keepdims=True)
        sel = key == first
        rank = jnp.where(sel, float(r), rank)
        s = jnp.where(sel, NEG_INF, s)
        vals_ref[r:r + 1, :] = m
    return rank


def _batcher_pairs(n):
    def merge(lo, hi, r):
        step = r * 2
        if step < hi - lo:
            yield from merge(lo, hi, step)
            yield from merge(lo + r, hi, step)
            yield from ((i, i + r) for i in range(lo + r, hi - r, step))
        else:
            yield (lo, lo + r)

    def sort(lo, hi):
        if hi - lo >= 1:
            mid = lo + (hi - lo) // 2
            yield from sort(lo, mid)
            yield from sort(mid + 1, hi)
            yield from merge(lo, hi, 1)

    return tuple(sort(0, n - 1))


SORT16_PAIRS = _batcher_pairs(PEER_TOPK)


def _top16_sorted(s, vals_ref):
    n = PEER_TOPK
    v = [s[i * SUBLANES:(i + 1) * SUBLANES, :] for i in range(n)]
    for i, j in SORT16_PAIRS:
        v[i], v[j] = jnp.maximum(v[i], v[j]), jnp.minimum(v[i], v[j])
    for shift in (4, 2, 1):
        v = [jnp.maximum(v[i], pltpu.roll(v[n - 1 - i], SUBLANES - shift, axis=0))
             for i in range(n)]
        for dist in (8, 4, 2, 1):
            for i in range(n):
                if i % (2 * dist) < dist:
                    j = i + dist
                    v[i], v[j] = jnp.maximum(v[i], v[j]), jnp.minimum(v[i], v[j])
    rows = [x[0:1, :] for x in v]
    for r in range(n):
        vals_ref[r:r + 1, :] = rows[r]
    count = jnp.sum(jnp.where(s >= rows[n - 1], 1.0, 0.0), axis=0, keepdims=True)
    for r in range(n - 1):
        count = count + jnp.where(rows[r] == rows[r + 1], 1.0, 0.0)
    return count


def _route_tile(s1, s2, v1_scr, v2_scr, ties):
    width = s1.shape[1]
    sub = lax.broadcasted_iota(jnp.int32, (SUBLANES, width), 0).astype(F32)
    if ties:
        rank1 = _top16_exact(s1, v1_scr)
        rank2 = _top16_exact(s2, v2_scr)
        marked1 = marked2 = jnp.full((1, width), float(PEER_TOPK), F32)
    else:
        marked1 = _top16_sorted(s1, v1_scr)
        marked2 = _top16_sorted(s2, v2_scr)
        rank2 = jnp.zeros(s2.shape, F32)
        for r in range(PEER_TOPK):
            rank2 = jnp.where(v2_scr[r:r + 1, :] > s2, float(r + 1), rank2)

    v2a = v2_scr[0:8, :]
    v2b = v2_scr[8:16, :]
    cands = [v1_scr[0:1, :] + v2a, v1_scr[0:1, :] + v2b]
    ids = [sub, sub + 8]
    for r in range(1, 8):
        cands.append(jnp.where(sub < CAND_COLS[r], v1_scr[r:r + 1, :] + v2a, NEG_INF))
        ids.append(sub + PEER_TOPK * r)
    cands.append(v1_scr[8:16, :] + v2_scr[0:1, :])
    ids.append((sub + 8) * PEER_TOPK)
    taken = [jnp.zeros((SUBLANES, width), F32) for _ in cands]
    mx = v1_scr[0:1, :] + v2_scr[0:1, :]
    zsum = jnp.zeros((1, width), F32)
    for _ in range(PEER_TOPK):
        m = functools.reduce(jnp.maximum, cands)
        m = jnp.max(m, axis=0, keepdims=True)
        sels = [c == m for c in cands]
        if ties:
            first = functools.reduce(
                jnp.minimum, [jnp.where(s, i, BIG_IDX) for s, i in zip(sels, ids)])
            first = jnp.min(first, axis=0, keepdims=True)
            sels = [i == first for i in ids]
        cands = [jnp.where(s, NEG_INF, c) for s, c in zip(sels, cands)]
        taken = [jnp.where(s, 1.0, t) for s, t in zip(sels, taken)]
        zsum = zsum + jnp.exp(m - mx)

    counts = [jnp.sum(taken[0] + taken[1], axis=0, keepdims=True)]
    for r in range(1, 8):
        counts.append(jnp.sum(taken[r + 1], axis=0, keepdims=True))
    for r in range(8, PEER_TOPK):
        counts.append(taken[9][r - 8:r - 7, :])
    n1 = jnp.zeros(s1.shape, F32)
    for r in range(PEER_TOPK):
        hit = (rank1 == r) if ties else (s1 == v1_scr[r:r + 1, :])
        n1 = jnp.where(hit, counts[r], n1)

    p1 = jnp.exp(s1 - v1_scr[0:1, :]) * (1.0 / zsum)
    p2 = jnp.exp(s2 - v2_scr[0:1, :])
    return n1, p1, rank2, p2, [marked1, marked2, functools.reduce(jnp.add, counts)]


def _route_kernel(h_ref, wq_ref, k1_ref, k2_ref, rank2_ref, p2_ref, n1_ref, p1_ref,
                  s1_scr, s2_scr, v1_scr, v2_scr):
    half = PEER_DQ // 2

    def scores(h):
        qt = _dot(wq_ref[h], h_ref[...])
        s1_scr[h] = _dot(k1_ref[h], qt[:half].astype(BF16))
        s2_scr[h] = _dot(k2_ref[h], qt[half:].astype(BF16))

    scores(0)

    def body(h, carry):
        s1 = s1_scr[h]
        s2 = s2_scr[h]
        scores(jnp.minimum(h + 1, PEER_HEADS - 1))

        def run(ties):
            n1, p1, rank2, p2, marked = _route_tile(s1, s2, v1_scr, v2_scr, ties)
            n1_ref[h] = n1
            p1_ref[h] = p1
            rank2_ref[h] = rank2.astype(rank2_ref.dtype)
            p2_ref[h] = p2.astype(p2_ref.dtype)
            return marked

        marked = run(ties=False)
        bad = functools.reduce(
            jnp.maximum, [jnp.where(mk == PEER_TOPK, 0.0, 1.0) for mk in marked])

        @pl.when(jnp.max(bad) > 0.0)
        def _():
            run(ties=True)

        return carry

    lax.fori_loop(0, PEER_HEADS, body, 0)


def _route(h2t, wq, k1, k2):
    n_t, d, tm = h2t.shape
    t = n_t * tm
    hk = (PEER_HEADS, N_KEYS, tm)
    tiled = (t // tm,) + hk
    out_spec = pl.BlockSpec((None,) + hk, lambda i: (i, 0, 0, 0))
    kspec = pl.BlockSpec((PEER_HEADS, N_KEYS, PEER_DQ // 2), lambda i: (0, 0, 0))
    wq_heads = wq.reshape(d, PEER_HEADS, PEER_DQ).transpose(1, 2, 0)
    return pl.pallas_call(
        _route_kernel,
        name="peer_route",
        grid=(t // tm,),
        in_specs=[
            pl.BlockSpec((None, d, tm), lambda i: (i, 0, 0)),
            pl.BlockSpec(wq_heads.shape, lambda i: (0, 0, 0)),
            kspec, kspec,
        ],
        out_specs=[out_spec, out_spec, out_spec, out_spec],
        out_shape=[
            jax.ShapeDtypeStruct(tiled, BF16),
            jax.ShapeDtypeStruct(tiled, BF16),
            jax.ShapeDtypeStruct(tiled, F32),
            jax.ShapeDtypeStruct(tiled, F32),
        ],
        scratch_shapes=[
            pltpu.VMEM(hk, F32), pltpu.VMEM(hk, F32),
            pltpu.VMEM((PEER_TOPK, tm), F32), pltpu.VMEM((PEER_TOPK, tm), F32),
        ],
        compiler_params=_cparams(("parallel",)),
    )(h2t, wq_heads, k1, k2)


def _peer_kernel(ht_ref, pu_ref, pvt_ref, rank2_ref, p2_ref, n1_ref, p1_ref, o_ref,
                 pa_scr, pb_scr, *, a_rows, n_e):
    j = pl.program_id(1)
    group = 1
    acc_split = 1
    n_groups = a_rows // group
    o_rows = o_ref.shape[0] // n_groups

    def activations(k, p_scr):
        base = pl.multiple_of(k * (group * N_KEYS), group * N_KEYS)
        scores = _dot(pu_ref[pl.ds(base, group * N_KEYS), :].astype(BF16), ht_ref[...])
        for g in range(group):
            a = j * a_rows + k * group + g
            w = None
            for h in range(PEER_HEADS):
                n1 = n1_ref[h, pl.ds(a, 1), :].astype(BF16)
                p1 = p1_ref[h, pl.ds(a, 1), :].astype(BF16)
                term = jnp.where(rank2_ref[h] < n1, p2_ref[h], jnp.zeros((), BF16)) * p1
                w = term if w is None else w + term
            act = _gelu(scores[g * N_KEYS:(g + 1) * N_KEYS, :])
            p_scr[pl.ds(base + g * N_KEYS, N_KEYS), :] = (act * w.astype(F32)).astype(BF16)

    def accumulate(k, p_scr):
        piece = o_rows // acc_split
        for s in range(acc_split):
            rows = pl.ds(pl.multiple_of(k * o_rows + s * piece, piece), piece)
            o_ref[rows, :] += _dot(pvt_ref[rows, :], p_scr[...])

    def both(p_new, p_old):
        def body(k, carry):
            activations(k, p_new)
            accumulate(k, p_old)
            return carry
        lax.fori_loop(0, n_groups, body, 0, unroll=n_groups)

    @pl.when(j == 0)
    def _():
        o_ref[...] = jnp.zeros(o_ref.shape, F32)
        lax.fori_loop(0, n_groups, lambda k, c: (activations(k, pa_scr), c)[1], 0)

    @pl.when((j > 0) & (j < n_e) & (j % 2 == 0))
    def _():
        both(pa_scr, pb_scr)

    @pl.when((j < n_e) & (j % 2 == 1))
    def _():
        both(pb_scr, pa_scr)

    @pl.when(j == n_e)
    def _():
        p_last = pb_scr if n_e % 2 == 0 else pa_scr
        lax.fori_loop(0, n_groups, lambda k, c: (accumulate(k, p_last), c)[1], 0)


PEER_TE = 512


def _peer_dense(ht, pu, pvt, rank2, p2, n1, p1):
    n_t, d, tm = ht.shape
    te = PEER_TE
    n_e = N_EXPERTS // te
    rspec = pl.BlockSpec((None, PEER_HEADS, N_KEYS, tm), lambda i, j: (i, 0, 0, 0))
    return pl.pallas_call(
        functools.partial(_peer_kernel, a_rows=te // N_KEYS, n_e=n_e),
        name="peer_dense",
        grid=(n_t, n_e + 1),
        in_specs=[
            pl.BlockSpec((None, d, tm), lambda i, j: (i, 0, 0)),
            pl.BlockSpec((te, d), lambda i, j: (jnp.minimum(j, n_e - 1), 0)),
            pl.BlockSpec((None, d, te), lambda i, j: (jnp.maximum(j - 1, 0), 0, 0)),
            rspec, rspec, rspec, rspec,
        ],
        out_specs=pl.BlockSpec((None, d, tm), lambda i, j: (i, 0, 0)),
        out_shape=jax.ShapeDtypeStruct((n_t, d, tm), F32),
        scratch_shapes=[pltpu.VMEM((te, tm), BF16), pltpu.VMEM((te, tm), BF16)],
        compiler_params=_cparams(("parallel", "arbitrary")),
    )(ht, pu, pvt, rank2, p2, n1, p1)


def _resid_kernel(x_ref, yt_ref, gt_ref, g_ref, o_ref, *, final):
    x2 = x_ref[...] + gt_ref[...] * yt_ref[...].T
    if final:
        x2 = x2 * lax.rsqrt(jnp.mean(x2 * x2, axis=-1, keepdims=True) + EPS) * g_ref[...]
    o_ref[...] = x2


def _resid(x1, yt, gt2, final_g, seq, final):
    t, d = x1.shape
    tm = min(256, seq)
    per_b = seq // tm
    per_p = yt.shape[2] // tm
    return pl.pallas_call(
        functools.partial(_resid_kernel, final=final),
        name="peer_resid",
        grid=(t // tm,),
        in_specs=[
            pl.BlockSpec((tm, d), lambda i: (i, 0)),
            pl.BlockSpec((None, d, tm), lambda i: (i // per_p, 0, i % per_p)),
            pl.BlockSpec((None, 1, d), lambda i: (i // per_b, 0, 0)),
            pl.BlockSpec((1, d), lambda i: (0, 0)),
        ],
        out_specs=pl.BlockSpec((tm, d), lambda i: (i, 0)),
        out_shape=jax.ShapeDtypeStruct((t, d), F32),
        compiler_params=_cparams(("parallel",)),
    )(x1, yt, gt2, final_g)


def _layout_w_in(w_in_l):
    d = w_in_l.shape[0]
    o_a = 2 * GLA_QK + 2 * GLA_V
    o_u = o_a + GLA_RANK
    pad = jnp.zeros((d, PROJ_COLS - (w_in_l.shape[1] - GLA_RANK) - GLA_RANK), w_in_l.dtype)
    w = jnp.concatenate([w_in_l[:, :o_a], w_in_l[:, o_u:], w_in_l[:, o_a:o_u], pad], axis=1)
    w = w.astype(BF16).reshape(d, PROJ_COLS // PROJ_TN, PROJ_TN)
    return w.transpose(1, 0, 2)


def kernel(x, c, ada_w, ada_b, norm1_g, w_in, gla_w_a2, gla_b_a, gla_norm_g, gmlp_vnorm_g,
           gmlp_ws, gmlp_b, gmlp_out_g, w_out, norm2_g, peer_wq, peer_k1, peer_k2, peer_u,
           peer_v, final_g):
    batch, seq, d = x.shape
    depth = ada_w.shape[0]
    t = batch * seq
    xf = x.reshape(t, d)

    mod = _modulation(c, ada_w, ada_b)
    causal = jnp.tril(jnp.ones((GMLP_CHUNK, GMLP_CHUNK), F32))

    for l in range(depth):
        sh1, sc1, gt1, sh2, sc2, gt2 = [m.reshape(batch, 1, d) for m in jnp.split(mod[l], 6, axis=-1)]

        proj = _inproj(xf, norm1_g[l].reshape(1, d), sc1, sh1, _layout_w_in(w_in[l]), seq)

        wa2p = jnp.zeros((LANES, GLA_QK), F32).at[:GLA_RANK].set(gla_w_a2[l])
        y_gla = _gla(proj.reshape(batch, seq, PROJ_COLS), wa2p, gla_b_a[l].reshape(1, GLA_QK),
                     gla_norm_g[l].reshape(1, GLA_V), batch, seq).reshape(t, GLA_V)

        bias_full = jnp.repeat(gmlp_b[l].T, GMLP_DH, axis=1)
        x1, h2t = _outproj(
            y_gla, proj, (gmlp_ws[l] * causal).astype(BF16), bias_full,
            gmlp_vnorm_g[l].reshape(1, GMLP_WIDTH), gmlp_out_g[l].reshape(1, GMLP_WIDTH),
            w_out[l].astype(BF16), xf, gt1, norm2_g[l].reshape(1, d), sc2, sh2, seq)

        rank2, p2, n1, p1 = _route(h2t, peer_wq[l].astype(BF16), peer_k1[l].astype(BF16),
                                   peer_k2[l].astype(BF16))
        pvt = peer_v[l].reshape(N_EXPERTS // PEER_TE, PEER_TE, d).transpose(0, 2, 1).astype(BF16)
        yt = _peer_dense(h2t, peer_u[l], pvt, rank2, p2, n1, p1)
        xf = _resid(x1, yt, gt2, final_g.reshape(1, d), seq, final=(l == depth - 1))

    return xf.reshape(batch, seq, d)
```

```python
import functools

import jax
import jax.numpy as jnp
from jax import lax
from jax.experimental import pallas as pl
from jax.experimental.pallas import tpu as pltpu

F32 = jnp.float32
BF16 = jnp.bfloat16

D_MODEL = 2048
EPS = 1e-6
GLA_HEADS = 4
GLA_DV = 256
GLA_DK = 128
GLA_QK = GLA_HEADS * GLA_DK
GLA_V = GLA_HEADS * GLA_DV
GLA_RANK = 16
GLA_CHUNK = 64
GMLP_HEADS = 8
GMLP_WIDTH = 1024
GMLP_DH = 128
GMLP_CHUNK = 128
N_KEYS = 128
N_EXPERTS = N_KEYS * N_KEYS
PEER_HEADS = 8
PEER_TOPK = 16
PEER_DQ = 256

LANES = 128
SUBLANES = 8
PROJ_COLS = 5376
COL_U = 3
COL_VSP = 4
COL_A = 40
VMEM_LIMIT = 56 * 1024 * 1024

NEG_INF = float("-inf")


def _cparams(sem):
    return pltpu.CompilerParams(dimension_semantics=sem, vmem_limit_bytes=VMEM_LIMIT)


def _gelu(x):
    c = 0.7978845608028654
    return 0.5 * x * (1.0 + jnp.tanh(c * (x + 0.044715 * (x * x * x))))


def _split3(x):
    hi = x.astype(BF16)
    r1 = x - hi.astype(F32)
    mid = r1.astype(BF16)
    lo = (r1 - mid.astype(F32)).astype(BF16)
    return hi, mid, lo


def _dot(a, b):
    return jnp.dot(a, b, preferred_element_type=F32)


def _dot_nt(a, b):
    return lax.dot_general(a, b, (((1,), (1,)), ((), ())), preferred_element_type=F32)


def _dot_tn(a, b):
    return lax.dot_general(a, b, (((0,), (0,)), ((), ())), preferred_element_type=F32)


MOD_STREAMS = 4


def _mod_kernel(cb_ref, *refs):
    w_refs = refs[:MOD_STREAMS]
    b_ref, o_ref = refs[MOD_STREAMS:]
    k = pl.program_id(1)
    part = w_refs[0].shape[1]
    c = cb_ref[...]
    cond = c / (1.0 + jnp.exp(-c))

    @pl.when(k == 0)
    def _():
        o_ref[...] = jnp.broadcast_to(b_ref[...], o_ref.shape)

    for q, w_ref in enumerate(w_refs):
        for j in range(part // LANES):
            w = w_ref[:, j * LANES:(j + 1) * LANES]
            cols = slice(q * part + j * LANES, q * part + (j + 1) * LANES)
            for b in range(cb_ref.shape[0]):
                o_ref[b:b + 1, cols] += jnp.sum(w * cond[b], axis=0, keepdims=True)


def _modulation(c, ada_w, ada_b):
    depth, d, n = ada_w.shape
    b = c.shape[0]
    tk = 128
    part = n // MOD_STREAMS
    cb = jnp.broadcast_to(c[:, :, None], (b, d, LANES))
    w_specs = [pl.BlockSpec((None, tk, part), lambda l, k, q=q: (l, k, q))
               for q in range(MOD_STREAMS)]
    return pl.pallas_call(
        _mod_kernel,
        name="adaln_mod",
        grid=(depth, d // tk),
        in_specs=[pl.BlockSpec((b, tk, LANES), lambda l, k: (0, k, 0))] + w_specs + [
            pl.BlockSpec((None, 1, n), lambda l, k: (l, 0, 0)),
        ],
        out_specs=pl.BlockSpec((None, b, n), lambda l, k: (l, 0, 0)),
        out_shape=jax.ShapeDtypeStruct((depth, b, n), F32),
        compiler_params=_cparams(("parallel", "arbitrary")),
    )(cb, *([ada_w] * MOD_STREAMS), ada_b.reshape(depth, 1, n))


def _inproj_kernel(x_ref, g_ref, sc_ref, sh_ref, w_ref, o_ref, h_scr):
    @pl.when(pl.program_id(1) == 0)
    def _():
        x = x_ref[...]
        ms = jnp.mean(x * x, axis=-1, keepdims=True)
        y = x * lax.rsqrt(ms + EPS) * g_ref[...]
        h_scr[...] = (y * (1.0 + sc_ref[...]) + sh_ref[...]).astype(BF16)

    o_ref[...] = _dot(h_scr[...], w_ref[pl.program_id(1)])


PROJ_TN = 1792


def _inproj(x, g, sc, sh, w, seq):
    t, d = x.shape
    n_tiles, _, tn = w.shape
    n = n_tiles * tn
    tm = min(512, seq)
    per_b = seq // tm
    return pl.pallas_call(
        _inproj_kernel,
        name="norm_inproj",
        grid=(t // tm, n_tiles),
        in_specs=[
            pl.BlockSpec((tm, d), lambda i, j: (i, 0)),
            pl.BlockSpec((1, d), lambda i, j: (0, 0)),
            pl.BlockSpec((None, 1, d), lambda i, j: (i // per_b, 0, 0)),
            pl.BlockSpec((None, 1, d), lambda i, j: (i // per_b, 0, 0)),
            pl.BlockSpec(w.shape, lambda i, j: (0, 0, 0), pipeline_mode=pl.Buffered(1)),
        ],
        out_specs=pl.BlockSpec((tm, tn), lambda i, j: (i, j)),
        out_shape=jax.ShapeDtypeStruct((t, n), F32),
        scratch_shapes=[pltpu.VMEM((tm, d), BF16)],
        compiler_params=_cparams(("parallel", "arbitrary")),
    )(x, g, sc, sh, w)


GLA_LEVELS = (32, 16, 8, 4, 2, 1)


def _gla_kernel(q_ref, k_ref, v_ref, r_ref, a_ref, wa_ref, ba_ref, g_ref, o_ref,
                state_ref, *, n_chunks):
    C = GLA_CHUNK

    @pl.when(pl.program_id(1) == 0)
    def _():
        state_ref[...] = jnp.zeros(state_ref.shape, F32)

    nl = len(GLA_LEVELS) + 1
    ri = lax.broadcasted_iota(jnp.int32, (C, C), 0)
    ci = lax.broadcasted_iota(jnp.int32, (C, C), 1)
    pieces = [ci <= ri]
    masks = [ci == ri]
    for s in GLA_LEVELS:
        blk_r = ri // (2 * s)
        bound = blk_r * (2 * s) + (s - 1)
        pieces.append(ci <= bound)
        masks.append((blk_r == ci // (2 * s)) & (ri % (2 * s) >= s) & (ci % (2 * s) < s))
    prefix = jnp.concatenate([jnp.where(p, 1.0, 0.0).astype(BF16) for p in pieces], axis=0)

    wa = wa_ref[...].astype(BF16)
    ba = ba_ref[...]
    gain = g_ref[...]
    scale = GLA_DK ** -0.5

    def chunk(c, carry):
        off = pl.multiple_of(c * C, C)
        rows = pl.ds(off, C)
        for b in range(q_ref.shape[0]):
            z = _dot(a_ref[b, rows, :].astype(BF16), wa) + ba
            la = -(jnp.maximum(-z, 0.0) + jnp.log1p(jnp.exp(-jnp.abs(z)))) * (1.0 / 16.0)
            hi, mid, lo = _split3(la)
            pref_all = _dot(prefix, hi) + _dot(prefix, mid) + _dot(prefix, lo)

            for h in range(GLA_HEADS):
                kcols = slice(h * GLA_DK, (h + 1) * GLA_DK)
                vcols = slice(h * GLA_DV, (h + 1) * GLA_DV)
                pref = pref_all[:, kcols]
                q = q_ref[b, rows, kcols] * scale
                k = k_ref[b, rows, kcols]
                v = v_ref[b, rows, vcols].astype(BF16)
                G = pref[0:C]
                g_last = G[C - 1:C, :]

                attn = jnp.where(masks[0], _dot_nt(q.astype(BF16), k.astype(BF16)), 0.0)
                for l in range(1, nl):
                    e = jnp.exp(-jnp.abs(G - pref[l * C:(l + 1) * C]))
                    a_l = _dot_nt((q * e).astype(BF16), (k * e).astype(BF16))
                    attn = attn + jnp.where(masks[l], a_l, 0.0)

                st = state_ref[b, h]
                o = (_dot(attn.astype(BF16), v)
                     + _dot_nt((q * jnp.exp(G)).astype(BF16), st.astype(BF16)))
                k_dec = (k * jnp.exp(g_last - G)).astype(BF16)
                state_ref[b, h] = st * jnp.exp(g_last) + _dot_tn(v, k_dec)

                y = o * lax.rsqrt(jnp.mean(o * o, axis=-1, keepdims=True) + EPS) * gain[:, vcols]
                r = r_ref[b, rows, vcols]
                o_ref[b, rows, vcols] = (y * (r / (1.0 + jnp.exp(-r)))).astype(o_ref.dtype)
        return carry

    lax.fori_loop(0, n_chunks, chunk, 0, unroll=2)


def _gla(proj, wa2p, ba, norm_g, batch, seq):
    lc = min(512, seq)
    nb = 2 if batch % 2 == 0 else 1
    blk = lambda width, col: pl.BlockSpec((nb, lc, width), lambda b, s: (b, s, col))
    return pl.pallas_call(
        functools.partial(_gla_kernel, n_chunks=lc // GLA_CHUNK),
        name="gla",
        grid=(batch // nb, seq // lc),
        in_specs=[
            blk(GLA_QK, 0),
            blk(GLA_QK, 1),
            blk(GLA_V, 1),
            blk(GLA_V, 2),
            blk(LANES, COL_A),
            pl.BlockSpec((LANES, GLA_QK), lambda b, s: (0, 0)),
            pl.BlockSpec((1, GLA_QK), lambda b, s: (0, 0)),
            pl.BlockSpec((1, GLA_V), lambda b, s: (0, 0)),
        ],
        out_specs=blk(GLA_V, 0),
        out_shape=jax.ShapeDtypeStruct((batch, seq, GLA_V), BF16),
        scratch_shapes=[pltpu.VMEM((nb, GLA_HEADS, GLA_DV, GLA_DK), F32)],
        compiler_params=_cparams(("parallel", "arbitrary")),
    )(proj, proj, proj, proj, proj, wa2p, ba, norm_g)


def _gmlp_chunk(u_ref, v_ref, ws_ref, bias_ref, vg_ref, og_ref, ym_scr, rows):
    for h in range(GMLP_HEADS):
        cols = slice(h * GMLP_DH, (h + 1) * GMLP_DH)
        v = _gelu(v_ref[rows, cols])
        v = v * lax.rsqrt(jnp.mean(v * v, axis=-1, keepdims=True) + EPS) * vg_ref[:, cols]
        sv = _dot(ws_ref[h], v.astype(BF16)) + bias_ref[:, cols]
        y = _gelu(u_ref[rows, cols]) * sv
        y = y * lax.rsqrt(jnp.mean(y * y, axis=-1, keepdims=True) + EPS) * og_ref[:, cols]
        ym_scr[rows, cols] = y.astype(ym_scr.dtype)


def _outproj_kernel(yg_ref, u_ref, v_ref, ws_ref, bias_ref, vg_ref, og_ref, wg_ref, wm_ref,
                    x_ref, gt_ref, g2_ref, sc_ref, sh_ref, x1_ref, h2t_ref, ym_scr):
    mix = _dot(yg_ref[...], wg_ref[...])
    for c in range(yg_ref.shape[0] // GMLP_CHUNK):
        _gmlp_chunk(u_ref, v_ref, ws_ref, bias_ref, vg_ref, og_ref, ym_scr,
                    slice(c * GMLP_CHUNK, (c + 1) * GMLP_CHUNK))
    mix = mix + _dot(ym_scr[...], wm_ref[...])
    x1 = x_ref[...] + gt_ref[...] * mix
    x1_ref[...] = x1
    y = x1 * lax.rsqrt(jnp.mean(x1 * x1, axis=-1, keepdims=True) + EPS) * g2_ref[...]
    h2 = y * (1.0 + sc_ref[...]) + sh_ref[...]
    h2t_ref[...] = h2.T.astype(BF16)


def _peer_tm(t):
    return min(512, t)


def _outproj(yg, proj, ws_causal, bias_full, vnorm_g, out_g, w_out, x, gt1, g2, sc2, sh2, seq):
    t, d = x.shape
    tm = min(256, seq)
    per_b = seq // tm
    half = d // 2
    ptm = _peer_tm(t)
    per_p = ptm // tm
    c = GMLP_CHUNK
    bvec = pl.BlockSpec((None, 1, d), lambda i: (i // per_b, 0, 0))
    return pl.pallas_call(
        _outproj_kernel,
        name="outproj",
        grid=(t // tm,),
        in_specs=[
            pl.BlockSpec((tm, half), lambda i: (i, 0)),
            pl.BlockSpec((tm, GMLP_WIDTH), lambda i: (i, COL_U)),
            pl.BlockSpec((tm, GMLP_WIDTH), lambda i: (i, COL_VSP)),
            pl.BlockSpec((GMLP_HEADS, c, c), lambda i: (0, 0, 0)),
            pl.BlockSpec((c, GMLP_WIDTH), lambda i: (0, 0)),
            pl.BlockSpec((1, GMLP_WIDTH), lambda i: (0, 0)),
            pl.BlockSpec((1, GMLP_WIDTH), lambda i: (0, 0)),
            pl.BlockSpec((half, d), lambda i: (0, 0)),
            pl.BlockSpec((half, d), lambda i: (1, 0)),
            pl.BlockSpec((tm, d), lambda i: (i, 0)),
            bvec,
            pl.BlockSpec((1, d), lambda i: (0, 0)),
            bvec,
            bvec,
        ],
        out_specs=[pl.BlockSpec((tm, d), lambda i: (i, 0)),
                   pl.BlockSpec((None, d, tm), lambda i: (i // per_p, 0, i % per_p))],
        out_shape=[jax.ShapeDtypeStruct((t, d), F32),
                   jax.ShapeDtypeStruct((t // ptm, d, ptm), BF16)],
        scratch_shapes=[pltpu.VMEM((tm, GMLP_WIDTH), BF16)],
        compiler_params=_cparams(("parallel",)),
    )(yg, proj, proj, ws_causal, bias_full, vnorm_g, out_g, w_out, w_out, x, gt1, g2, sc2, sh2)


CAND_COLS = tuple(PEER_TOPK // (r + 1) for r in range(PEER_TOPK))
BIG_IDX = float(1 << 20)


def _top16_exact(s, vals_ref):
    key = lax.broadcasted_iota(jnp.int32, s.shape, 0).astype(F32)
    rank = jnp.full(s.shape, float(PEER_TOPK), F32)
    for r in range(PEER_TOPK):
        m = jnp.max(s, axis=0, keepdims=True)
        first = jnp.min(jnp.where(s == m, key, BIG_IDX), axis=0, keepdims=True)
        sel = key == first
        rank = jnp.where(sel, float(r), rank)
        s = jnp.where(sel, NEG_INF, s)
        vals_ref[r:r + 1, :] = m
    return rank


def _batcher_pairs(n):
    def merge(lo, hi, r):
        step = r * 2
        if step < hi - lo:
            yield from merge(lo, hi, step)
            yield from merge(lo + r, hi, step)
            yield from ((i, i + r) for i in range(lo + r, hi - r, step))
        else:
            yield (lo, lo + r)

    def sort(lo, hi):
        if hi - lo >= 1:
            mid = lo + (hi - lo) // 2
            yield from sort(lo, mid)
            yield from sort(mid + 1, hi)
            yield from merge(lo, hi, 1)

    return tuple(sort(0, n - 1))


SORT16_PAIRS = _batcher_pairs(PEER_TOPK)


def _top16_sorted(s, vals_ref):
    n = PEER_TOPK
    v = [s[i * SUBLANES:(i + 1) * SUBLANES, :] for i in range(n)]
    for i, j in SORT16_PAIRS:
        v[i], v[j] = jnp.maximum(v[i], v[j]), jnp.minimum(v[i], v[j])
    for shift in (4, 2, 1):
        v = [jnp.maximum(v[i], pltpu.roll(v[n - 1 - i], SUBLANES - shift, axis=0))
             for i in range(n)]
        for dist in (8, 4, 2, 1):
            for i in range(n):
                if i % (2 * dist) < dist:
                    j = i + dist
                    v[i], v[j] = jnp.maximum(v[i], v[j]), jnp.minimum(v[i], v[j])
    rows = [x[0:1, :] for x in v]
    for r in range(n):
        vals_ref[r:r + 1, :] = rows[r]
    count = jnp.sum(jnp.where(s >= rows[n - 1], 1.0, 0.0), axis=0, keepdims=True)
    for r in range(n - 1):
        count = count + jnp.where(rows[r] == rows[r + 1], 1.0, 0.0)
    return count


def _route_tile(s1, s2, v1_scr, v2_scr, ties):
    width = s1.shape[1]
    sub = lax.broadcasted_iota(jnp.int32, (SUBLANES, width), 0).astype(F32)
    if ties:
        rank1 = _top16_exact(s1, v1_scr)
        rank2 = _top16_exact(s2, v2_scr)
        marked1 = marked2 = jnp.full((1, width), float(PEER_TOPK), F32)
    else:
        marked1 = _top16_sorted(s1, v1_scr)
        marked2 = _top16_sorted(s2, v2_scr)
        rank2 = jnp.zeros(s2.shape, F32)
        for r in range(PEER_TOPK):
            rank2 = jnp.where(v2_scr[r:r + 1, :] > s2, float(r + 1), rank2)

    v2a = v2_scr[0:8, :]
    v2b = v2_scr[8:16, :]
    cands = [v1_scr[0:1, :] + v2a, v1_scr[0:1, :] + v2b]
    ids = [sub, sub + 8]
    for r in range(1, 8):
        cands.append(jnp.where(sub < CAND_COLS[r], v1_scr[r:r + 1, :] + v2a, NEG_INF))
        ids.append(sub + PEER_TOPK * r)
    cands.append(v1_scr[8:16, :] + v2_scr[0:1, :])
    ids.append((sub + 8) * PEER_TOPK)
    taken = [jnp.zeros((SUBLANES, width), F32) for _ in cands]
    mx = v1_scr[0:1, :] + v2_scr[0:1, :]
    zsum = jnp.zeros((1, width), F32)
    for _ in range(PEER_TOPK):
        m = functools.reduce(jnp.maximum, cands)
        m = jnp.max(m, axis=0, keepdims=True)
        sels = [c == m for c in cands]
        if ties:
            first = functools.reduce(
                jnp.minimum, [jnp.where(s, i, BIG_IDX) for s, i in zip(sels, ids)])
            first = jnp.min(first, axis=0, keepdims=True)
            sels = [i == first for i in ids]
        cands = [jnp.where(s, NEG_INF, c) for s, c in zip(sels, cands)]
        taken = [jnp.where(s, 1.0, t) for s, t in zip(sels, taken)]
        zsum = zsum + jnp.exp(m - mx)

    counts = [jnp.sum(taken[0] + taken[1], axis=0, keepdims=True)]
    for r in range(1, 8):
        counts.append(jnp.sum(taken[r + 1], axis=0, keepdims=True))
    for r in range(8, PEER_TOPK):
        counts.append(taken[9][r - 8:r - 7, :])
    n1 = jnp.zeros(s1.shape, F32)
    for r in range(PEER_TOPK):
        hit = (rank1 == r) if ties else (s1 == v1_scr[r:r + 1, :])
        n1 = jnp.where(hit, counts[r], n1)

    p1 = jnp.exp(s1 - v1_scr[0:1, :]) * (1.0 / zsum)
    p2 = jnp.exp(s2 - v2_scr[0:1, :])
    return n1, p1, rank2, p2, [marked1, marked2, functools.reduce(jnp.add, counts)]


def _route_kernel(h_ref, wq_ref, k1_ref, k2_ref, rank2_ref, p2_ref, n1_ref, p1_ref,
                  s1_scr, s2_scr, v1_scr, v2_scr):
    half = PEER_DQ // 2

    def scores(h):
        qt = _dot(wq_ref[h], h_ref[...])
        s1_scr[h] = _dot(k1_ref[h], qt[:half].astype(BF16))
        s2_scr[h] = _dot(k2_ref[h], qt[half:].astype(BF16))

    scores(0)

    def body(h, carry):
        s1 = s1_scr[h]
        s2 = s2_scr[h]
        scores(jnp.minimum(h + 1, PEER_HEADS - 1))

        def run(ties):
            n1, p1, rank2, p2, marked = _route_tile(s1, s2, v1_scr, v2_scr, ties)
            n1_ref[h] = n1
            p1_ref[h] = p1
            rank2_ref[h] = rank2.astype(rank2_ref.dtype)
            p2_ref[h] = p2.astype(p2_ref.dtype)
            return marked

        marked = run(ties=False)
        bad = functools.reduce(
            jnp.maximum, [jnp.where(mk == PEER_TOPK, 0.0, 1.0) for mk in marked])

        @pl.when(jnp.max(bad) > 0.0)
        def _():
            run(ties=True)

        return carry

    lax.fori_loop(0, PEER_HEADS, body, 0)


def _route(h2t, wq, k1, k2):
    n_t, d, tm = h2t.shape
    t = n_t * tm
    hk = (PEER_HEADS, N_KEYS, tm)
    tiled = (t // tm,) + hk
    out_spec = pl.BlockSpec((None,) + hk, lambda i: (i, 0, 0, 0))
    kspec = pl.BlockSpec((PEER_HEADS, N_KEYS, PEER_DQ // 2), lambda i: (0, 0, 0))
    wq_heads = wq.reshape(d, PEER_HEADS, PEER_DQ).transpose(1, 2, 0)
    return pl.pallas_call(
        _route_kernel,
        name="peer_route",
        grid=(t // tm,),
        in_specs=[
            pl.BlockSpec((None, d, tm), lambda i: (i, 0, 0)),
            pl.BlockSpec(wq_heads.shape, lambda i: (0, 0, 0)),
            kspec, kspec,
        ],
        out_specs=[out_spec, out_spec, out_spec, out_spec],
        out_shape=[
            jax.ShapeDtypeStruct(tiled, BF16),
            jax.ShapeDtypeStruct(tiled, BF16),
            jax.ShapeDtypeStruct(tiled, F32),
            jax.ShapeDtypeStruct(tiled, F32),
        ],
        scratch_shapes=[
            pltpu.VMEM(hk, F32), pltpu.VMEM(hk, F32),
            pltpu.VMEM((PEER_TOPK, tm), F32), pltpu.VMEM((PEER_TOPK, tm), F32),
        ],
        compiler_params=_cparams(("parallel",)),
    )(h2t, wq_heads, k1, k2)


PEER_TE = 512
PEER_A_ROWS = PEER_TE // N_KEYS


def _peer_kernel(*refs, n_e):
    ht_ref = refs[0]
    pu_refs = refs[1:1 + PEER_A_ROWS]
    pvt_ref, rank2_ref, p2_ref, n1_ref, p1_ref, o_ref, pa_scr, pb_scr = refs[1 + PEER_A_ROWS:]
    j = pl.program_id(1)
    o_rows = o_ref.shape[0] // PEER_A_ROWS

    def activations(k, p_scr):
        scores = _dot(pu_refs[k][...].astype(BF16), ht_ref[...])
        a = j * PEER_A_ROWS + k
        w = None
        for h in range(PEER_HEADS):
            n1 = n1_ref[h, pl.ds(a, 1), :].astype(BF16)
            p1 = p1_ref[h, pl.ds(a, 1), :].astype(BF16)
            term = jnp.where(rank2_ref[h] < n1, p2_ref[h], jnp.zeros((), BF16)) * p1
            w = term if w is None else w + term
        rows = slice(k * N_KEYS, (k + 1) * N_KEYS)
        p_scr[rows, :] = (_gelu(scores) * w.astype(F32)).astype(BF16)

    def accumulate(k, p_scr):
        rows = slice(k * o_rows, (k + 1) * o_rows)
        o_ref[rows, :] += _dot(pvt_ref[rows, :], p_scr[...])

    def both(p_new, p_old):
        for k in range(PEER_A_ROWS):
            activations(k, p_new)
            accumulate(k, p_old)

    @pl.when(j == 0)
    def _():
        o_ref[...] = jnp.zeros(o_ref.shape, F32)
        for k in range(PEER_A_ROWS):
            activations(k, pa_scr)

    @pl.when((j > 0) & (j < n_e) & (j % 2 == 0))
    def _():
        both(pa_scr, pb_scr)

    @pl.when((j < n_e) & (j % 2 == 1))
    def _():
        both(pb_scr, pa_scr)

    @pl.when(j == n_e)
    def _():
        p_last = pb_scr if n_e % 2 == 0 else pa_scr
        for k in range(PEER_A_ROWS):
            accumulate(k, p_last)


def _peer_dense(ht, pu_all, layer, pvt, rank2, p2, n1, p1):
    n_t, d, tm = ht.shape
    te = PEER_TE
    n_e = N_EXPERTS // te
    rspec = pl.BlockSpec((None, PEER_HEADS, N_KEYS, tm), lambda i, j: (i, 0, 0, 0))
    pu_specs = [
        pl.BlockSpec((None, N_KEYS, d),
                     lambda i, j, k=k: (layer, jnp.minimum(j, n_e - 1) * PEER_A_ROWS + k, 0))
        for k in range(PEER_A_ROWS)]
    return pl.pallas_call(
        functools.partial(_peer_kernel, n_e=n_e),
        name="peer_dense",
        grid=(n_t, n_e + 1),
        in_specs=[pl.BlockSpec((None, d, tm), lambda i, j: (i, 0, 0))] + pu_specs + [
            pl.BlockSpec((None, d, te), lambda i, j: (jnp.maximum(j - 1, 0), 0, 0)),
            rspec, rspec, rspec, rspec,
        ],
        out_specs=pl.BlockSpec((None, d, tm), lambda i, j: (i, 0, 0)),
        out_shape=jax.ShapeDtypeStruct((n_t, d, tm), F32),
        scratch_shapes=[pltpu.VMEM((te, tm), BF16), pltpu.VMEM((te, tm), BF16)],
        compiler_params=_cparams(("parallel", "arbitrary")),
    )(ht, *([pu_all] * PEER_A_ROWS), pvt, rank2, p2, n1, p1)


def _resid_kernel(x_ref, yt_ref, gt_ref, g_ref, o_ref, *, final):
    x2 = x_ref[...] + gt_ref[...] * yt_ref[...].T
    if final:
        x2 = x2 * lax.rsqrt(jnp.mean(x2 * x2, axis=-1, keepdims=True) + EPS) * g_ref[...]
    o_ref[...] = x2


def _resid(x1, yt, gt2, final_g, seq, final):
    t, d = x1.shape
    tm = min(256, seq)
    per_b = seq // tm
    per_p = yt.shape[2] // tm
    return pl.pallas_call(
        functools.partial(_resid_kernel, final=final),
        name="peer_resid",
        grid=(t // tm,),
        in_specs=[
            pl.BlockSpec((tm, d), lambda i: (i, 0)),
            pl.BlockSpec((None, d, tm), lambda i: (i // per_p, 0, i % per_p)),
            pl.BlockSpec((None, 1, d), lambda i: (i // per_b, 0, 0)),
            pl.BlockSpec((1, d), lambda i: (0, 0)),
        ],
        out_specs=pl.BlockSpec((tm, d), lambda i: (i, 0)),
        out_shape=jax.ShapeDtypeStruct((t, d), F32),
        compiler_params=_cparams(("parallel",)),
    )(x1, yt, gt2, final_g)


def _layout_w_in(w_in_l):
    d = w_in_l.shape[0]
    o_a = 2 * GLA_QK + 2 * GLA_V
    o_u = o_a + GLA_RANK
    pad = jnp.zeros((d, PROJ_COLS - (w_in_l.shape[1] - GLA_RANK) - GLA_RANK), w_in_l.dtype)
    w = jnp.concatenate([w_in_l[:, :o_a], w_in_l[:, o_u:], w_in_l[:, o_a:o_u], pad], axis=1)
    w = w.astype(BF16).reshape(d, PROJ_COLS // PROJ_TN, PROJ_TN)
    return w.transpose(1, 0, 2)


def kernel(x, c, ada_w, ada_b, norm1_g, w_in, gla_w_a2, gla_b_a, gla_norm_g, gmlp_vnorm_g,
           gmlp_ws, gmlp_b, gmlp_out_g, w_out, norm2_g, peer_wq, peer_k1, peer_k2, peer_u,
           peer_v, final_g):
    batch, seq, d = x.shape
    depth = ada_w.shape[0]
    t = batch * seq
    xf = x.reshape(t, d)

    mod = _modulation(c, ada_w, ada_b)
    causal = jnp.tril(jnp.ones((GMLP_CHUNK, GMLP_CHUNK), F32))

    for l in range(depth):
        sh1, sc1, gt1, sh2, sc2, gt2 = [m.reshape(batch, 1, d) for m in jnp.split(mod[l], 6, axis=-1)]

        proj = _inproj(xf, norm1_g[l].reshape(1, d), sc1, sh1, _layout_w_in(w_in[l]), seq)

        wa2p = jnp.zeros((LANES, GLA_QK), F32).at[:GLA_RANK].set(gla_w_a2[l])
        y_gla = _gla(proj.reshape(batch, seq, PROJ_COLS), wa2p, gla_b_a[l].reshape(1, GLA_QK),
                     gla_norm_g[l].reshape(1, GLA_V), batch, seq).reshape(t, GLA_V)

        bias_full = jnp.repeat(gmlp_b[l].T, GMLP_DH, axis=1)
        x1, h2t = _outproj(
            y_gla, proj, (gmlp_ws[l] * causal).astype(BF16), bias_full,
            gmlp_vnorm_g[l].reshape(1, GMLP_WIDTH), gmlp_out_g[l].reshape(1, GMLP_WIDTH),
            w_out[l].astype(BF16), xf, gt1, norm2_g[l].reshape(1, d), sc2, sh2, seq)

        rank2, p2, n1, p1 = _route(h2t, peer_wq[l].astype(BF16), peer_k1[l].astype(BF16),
                                   peer_k2[l].astype(BF16))
        pvt = peer_v[l].reshape(N_EXPERTS // PEER_TE, PEER_TE, d).transpose(0, 2, 1).astype(BF16)
        yt = _peer_dense(h2t, peer_u, l, pvt, rank2, p2, n1, p1)
        xf = _resid(x1, yt, gt2, final_g.reshape(1, d), seq, final=(l == depth - 1))

    return xf.reshape(batch, seq, d)
```

```python
import functools

import jax
import jax.numpy as jnp
from jax import lax
from jax.experimental import pallas as pl
from jax.experimental.pallas import tpu as pltpu

F32 = jnp.float32
BF16 = jnp.bfloat16

D_MODEL = 2048
EPS = 1e-6
GLA_HEADS = 4
GLA_DV = 256
GLA_DK = 128
GLA_QK = GLA_HEADS * GLA_DK
GLA_V = GLA_HEADS * GLA_DV
GLA_RANK = 16
GLA_CHUNK = 64
GMLP_HEADS = 8
GMLP_WIDTH = 1024
GMLP_DH = 128
GMLP_CHUNK = 128
N_KEYS = 128
N_EXPERTS = N_KEYS * N_KEYS
PEER_HEADS = 8
PEER_TOPK = 16
PEER_DQ = 256

LANES = 128
SUBLANES = 8
PROJ_COLS = 5376
COL_U = 3
COL_VSP = 4
COL_A = 40
VMEM_LIMIT = 56 * 1024 * 1024

NEG_INF = float("-inf")


def _cparams(sem):
    return pltpu.CompilerParams(dimension_semantics=sem, vmem_limit_bytes=VMEM_LIMIT)


def _gelu(x):
    c = 0.7978845608028654
    return 0.5 * x * (1.0 + jnp.tanh(c * (x + 0.044715 * (x * x * x))))


def _split3(x):
    hi = x.astype(BF16)
    r1 = x - hi.astype(F32)
    mid = r1.astype(BF16)
    lo = (r1 - mid.astype(F32)).astype(BF16)
    return hi, mid, lo


def _dot(a, b):
    return jnp.dot(a, b, preferred_element_type=F32)


def _dot_nt(a, b):
    return lax.dot_general(a, b, (((1,), (1,)), ((), ())), preferred_element_type=F32)


def _dot_tn(a, b):
    return lax.dot_general(a, b, (((0,), (0,)), ((), ())), preferred_element_type=F32)


def _mod_kernel(cb_ref, w_ref, b_ref, o_ref, cond_scr):
    @pl.when((pl.program_id(0) == 0) & (pl.program_id(1) == 0))
    def _():
        c = cb_ref[...]
        cond_scr[...] = c / (1.0 + jnp.exp(-c))

    tn = w_ref.shape[1]
    for b in range(cb_ref.shape[0]):
        cond = cond_scr[b]
        cols = [jnp.sum(w_ref[:, j * LANES:(j + 1) * LANES] * cond, axis=0, keepdims=True)
                for j in range(tn // LANES)]
        o_ref[b:b + 1, :] = jnp.concatenate(cols, axis=1) + b_ref[...]


def _modulation(c, ada_w, ada_b):
    depth, d, n = ada_w.shape
    b = c.shape[0]
    tn = 512
    cb = jnp.broadcast_to(c[:, :, None], (b, d, LANES))
    return pl.pallas_call(
        _mod_kernel,
        name="adaln_mod",
        grid=(depth, n // tn),
        in_specs=[
            pl.BlockSpec((b, d, LANES), lambda l, j: (0, 0, 0)),
            pl.BlockSpec((None, d, tn), lambda l, j: (l, 0, j)),
            pl.BlockSpec((None, 1, tn), lambda l, j: (l, 0, j)),
        ],
        out_specs=pl.BlockSpec((None, b, tn), lambda l, j: (l, 0, j)),
        out_shape=jax.ShapeDtypeStruct((depth, b, n), F32),
        scratch_shapes=[pltpu.VMEM((b, d, LANES), F32)],
        compiler_params=_cparams(("arbitrary", "arbitrary")),
    )(cb, ada_w, ada_b.reshape(depth, 1, n))


def _inproj_kernel(x_ref, g_ref, sc_ref, sh_ref, w_ref, o_ref, h_scr):
    @pl.when(pl.program_id(1) == 0)
    def _():
        x = x_ref[...]
        ms = jnp.mean(x * x, axis=-1, keepdims=True)
        y = x * lax.rsqrt(ms + EPS) * g_ref[...]
        h_scr[...] = (y * (1.0 + sc_ref[...]) + sh_ref[...]).astype(BF16)

    o_ref[...] = _dot(h_scr[...], w_ref[pl.program_id(1)])


PROJ_TN = 1792


def _inproj(x, g, sc, sh, w, seq):
    t, d = x.shape
    n_tiles, _, tn = w.shape
    n = n_tiles * tn
    tm = min(512, seq)
    per_b = seq // tm
    return pl.pallas_call(
        _inproj_kernel,
        name="norm_inproj",
        grid=(t // tm, n_tiles),
        in_specs=[
            pl.BlockSpec((tm, d), lambda i, j: (i, 0)),
            pl.BlockSpec((1, d), lambda i, j: (0, 0)),
            pl.BlockSpec((None, 1, d), lambda i, j: (i // per_b, 0, 0)),
            pl.BlockSpec((None, 1, d), lambda i, j: (i // per_b, 0, 0)),
            pl.BlockSpec(w.shape, lambda i, j: (0, 0, 0), pipeline_mode=pl.Buffered(1)),
        ],
        out_specs=pl.BlockSpec((tm, tn), lambda i, j: (i, j)),
        out_shape=jax.ShapeDtypeStruct((t, n), F32),
        scratch_shapes=[pltpu.VMEM((tm, d), BF16)],
        compiler_params=_cparams(("parallel", "arbitrary")),
    )(x, g, sc, sh, w)


GLA_LEVELS = (32, 16, 8, 4, 2, 1)


def _gla_kernel(q_ref, k_ref, v_ref, r_ref, a_ref, wa_ref, ba_ref, g_ref, o_ref,
                state_ref, *, n_chunks):
    C = GLA_CHUNK

    @pl.when(pl.program_id(1) == 0)
    def _():
        state_ref[...] = jnp.zeros(state_ref.shape, F32)

    nl = len(GLA_LEVELS) + 1
    ri = lax.broadcasted_iota(jnp.int32, (C, C), 0)
    ci = lax.broadcasted_iota(jnp.int32, (C, C), 1)
    pieces = [ci <= ri]
    masks = [ci == ri]
    for s in GLA_LEVELS:
        blk_r = ri // (2 * s)
        bound = blk_r * (2 * s) + (s - 1)
        pieces.append(ci <= bound)
        masks.append((blk_r == ci // (2 * s)) & (ri % (2 * s) >= s) & (ci % (2 * s) < s))
    prefix = jnp.concatenate([jnp.where(p, 1.0, 0.0).astype(BF16) for p in pieces], axis=0)

    wa = wa_ref[...].astype(BF16)
    ba = ba_ref[...]
    gain = g_ref[...]
    scale = GLA_DK ** -0.5

    def chunk(c, carry):
        off = pl.multiple_of(c * C, C)
        rows = pl.ds(off, C)
        for b in range(q_ref.shape[0]):
            z = _dot(a_ref[b, rows, :].astype(BF16), wa) + ba
            la = -(jnp.maximum(-z, 0.0) + jnp.log1p(jnp.exp(-jnp.abs(z)))) * (1.0 / 16.0)
            hi, mid, lo = _split3(la)
            pref_all = _dot(prefix, hi) + _dot(prefix, mid) + _dot(prefix, lo)

            for h in range(GLA_HEADS):
                kcols = slice(h * GLA_DK, (h + 1) * GLA_DK)
                vcols = slice(h * GLA_DV, (h + 1) * GLA_DV)
                pref = pref_all[:, kcols]
                q = q_ref[b, rows, kcols] * scale
                k = k_ref[b, rows, kcols]
                v = v_ref[b, rows, vcols].astype(BF16)
                G = pref[0:C]
                g_last = G[C - 1:C, :]

                attn = jnp.where(masks[0], _dot_nt(q.astype(BF16), k.astype(BF16)), 0.0)
                for l in range(1, nl):
                    e = jnp.exp(-jnp.abs(G - pref[l * C:(l + 1) * C]))
                    a_l = _dot_nt((q * e).astype(BF16), (k * e).astype(BF16))
                    attn = attn + jnp.where(masks[l], a_l, 0.0)

                st = state_ref[b, h]
                o = (_dot(attn.astype(BF16), v)
                     + _dot_nt((q * jnp.exp(G)).astype(BF16), st.astype(BF16)))
                k_dec = (k * jnp.exp(g_last - G)).astype(BF16)
                state_ref[b, h] = st * jnp.exp(g_last) + _dot_tn(v, k_dec)

                y = o * lax.rsqrt(jnp.mean(o * o, axis=-1, keepdims=True) + EPS) * gain[:, vcols]
                r = r_ref[b, rows, vcols]
                o_ref[b, rows, vcols] = (y * (r / (1.0 + jnp.exp(-r)))).astype(o_ref.dtype)
        return carry

    lax.fori_loop(0, n_chunks, chunk, 0, unroll=2)


def _gla(proj, wa2p, ba, norm_g, batch, seq):
    lc = min(512, seq)
    nb = 2 if batch % 2 == 0 else 1
    blk = lambda width, col: pl.BlockSpec((nb, lc, width), lambda b, s: (b, s, col))
    return pl.pallas_call(
        functools.partial(_gla_kernel, n_chunks=lc // GLA_CHUNK),
        name="gla",
        grid=(batch // nb, seq // lc),
        in_specs=[
            blk(GLA_QK, 0),
            blk(GLA_QK, 1),
            blk(GLA_V, 1),
            blk(GLA_V, 2),
            blk(LANES, COL_A),
            pl.BlockSpec((LANES, GLA_QK), lambda b, s: (0, 0)),
            pl.BlockSpec((1, GLA_QK), lambda b, s: (0, 0)),
            pl.BlockSpec((1, GLA_V), lambda b, s: (0, 0)),
        ],
        out_specs=blk(GLA_V, 0),
        out_shape=jax.ShapeDtypeStruct((batch, seq, GLA_V), BF16),
        scratch_shapes=[pltpu.VMEM((nb, GLA_HEADS, GLA_DV, GLA_DK), F32)],
        compiler_params=_cparams(("parallel", "arbitrary")),
    )(proj, proj, proj, proj, proj, wa2p, ba, norm_g)


def _gmlp_chunk(u_ref, v_ref, ws_ref, bias_ref, vg_ref, og_ref, ym_scr, rows):
    for h in range(GMLP_HEADS):
        cols = slice(h * GMLP_DH, (h + 1) * GMLP_DH)
        v = _gelu(v_ref[rows, cols])
        v = v * lax.rsqrt(jnp.mean(v * v, axis=-1, keepdims=True) + EPS) * vg_ref[:, cols]
        sv = _dot(ws_ref[h], v.astype(BF16)) + bias_ref[:, cols]
        y = _gelu(u_ref[rows, cols]) * sv
        y = y * lax.rsqrt(jnp.mean(y * y, axis=-1, keepdims=True) + EPS) * og_ref[:, cols]
        ym_scr[rows, cols] = y.astype(ym_scr.dtype)


def _outproj_kernel(yg_ref, u_ref, v_ref, ws_ref, bias_ref, vg_ref, og_ref, wg_ref, wm_ref,
                    x_ref, gt_ref, g2_ref, sc_ref, sh_ref, x1_ref, h2t_ref, ym_scr):
    mix = _dot(yg_ref[...], wg_ref[...])
    for c in range(yg_ref.shape[0] // GMLP_CHUNK):
        _gmlp_chunk(u_ref, v_ref, ws_ref, bias_ref, vg_ref, og_ref, ym_scr,
                    slice(c * GMLP_CHUNK, (c + 1) * GMLP_CHUNK))
    mix = mix + _dot(ym_scr[...], wm_ref[...])
    x1 = x_ref[...] + gt_ref[...] * mix
    x1_ref[...] = x1
    y = x1 * lax.rsqrt(jnp.mean(x1 * x1, axis=-1, keepdims=True) + EPS) * g2_ref[...]
    h2 = y * (1.0 + sc_ref[...]) + sh_ref[...]
    h2t_ref[...] = h2.T.astype(BF16)


def _peer_tm(t):
    return min(512, t)


def _outproj(yg, proj, ws_causal, bias_full, vnorm_g, out_g, w_out, x, gt1, g2, sc2, sh2, seq):
    t, d = x.shape
    tm = min(256, seq)
    per_b = seq // tm
    half = d // 2
    ptm = _peer_tm(t)
    per_p = ptm // tm
    c = GMLP_CHUNK
    bvec = pl.BlockSpec((None, 1, d), lambda i: (i // per_b, 0, 0))
    return pl.pallas_call(
        _outproj_kernel,
        name="outproj",
        grid=(t // tm,),
        in_specs=[
            pl.BlockSpec((tm, half), lambda i: (i, 0)),
            pl.BlockSpec((tm, GMLP_WIDTH), lambda i: (i, COL_U)),
            pl.BlockSpec((tm, GMLP_WIDTH), lambda i: (i, COL_VSP)),
            pl.BlockSpec((GMLP_HEADS, c, c), lambda i: (0, 0, 0)),
            pl.BlockSpec((c, GMLP_WIDTH), lambda i: (0, 0)),
            pl.BlockSpec((1, GMLP_WIDTH), lambda i: (0, 0)),
            pl.BlockSpec((1, GMLP_WIDTH), lambda i: (0, 0)),
            pl.BlockSpec((half, d), lambda i: (0, 0)),
            pl.BlockSpec((half, d), lambda i: (1, 0)),
            pl.BlockSpec((tm, d), lambda i: (i, 0)),
            bvec,
            pl.BlockSpec((1, d), lambda i: (0, 0)),
            bvec,
            bvec,
        ],
        out_specs=[pl.BlockSpec((tm, d), lambda i: (i, 0)),
                   pl.BlockSpec((None, d, tm), lambda i: (i // per_p, 0, i % per_p))],
        out_shape=[jax.ShapeDtypeStruct((t, d), F32),
                   jax.ShapeDtypeStruct((t // ptm, d, ptm), BF16)],
        scratch_shapes=[pltpu.VMEM((tm, GMLP_WIDTH), BF16)],
        compiler_params=_cparams(("parallel",)),
    )(yg, proj, proj, ws_causal, bias_full, vnorm_g, out_g, w_out, w_out, x, gt1, g2, sc2, sh2)


CAND_COLS = tuple(PEER_TOPK // (r + 1) for r in range(PEER_TOPK))
BIG_IDX = float(1 << 20)


def _top16_exact(s, vals_ref):
    key = lax.broadcasted_iota(jnp.int32, s.shape, 0).astype(F32)
    rank = jnp.full(s.shape, float(PEER_TOPK), F32)
    for r in range(PEER_TOPK):
        m = jnp.max(s, axis=0, keepdims=True)
        first = jnp.min(jnp.where(s == m, key, BIG_IDX), axis=0, keepdims=True)
        sel = key == first
        rank = jnp.where(sel, float(r), rank)
        s = jnp.where(sel, NEG_INF, s)
        vals_ref[r:r + 1, :] = m
    return rank


def _batcher_pairs(n):
    def merge(lo, hi, r):
        step = r * 2
        if step < hi - lo:
            yield from merge(lo, hi, step)
            yield from merge(lo + r, hi, step)
            yield from ((i, i + r) for i in range(lo + r, hi - r, step))
        else:
            yield (lo, lo + r)

    def sort(lo, hi):
        if hi - lo >= 1:
            mid = lo + (hi - lo) // 2
            yield from sort(lo, mid)
            yield from sort(mid + 1, hi)
            yield from merge(lo, hi, 1)

    return tuple(sort(0, n - 1))


SORT16_PAIRS = _batcher_pairs(PEER_TOPK)


def _top16_sorted(s, vals_ref):
    n = PEER_TOPK
    v = [s[i * SUBLANES:(i + 1) * SUBLANES, :] for i in range(n)]
    for i, j in SORT16_PAIRS:
        v[i], v[j] = jnp.maximum(v[i], v[j]), jnp.minimum(v[i], v[j])
    for shift in (4, 2, 1):
        v = [jnp.maximum(v[i], pltpu.roll(v[n - 1 - i], SUBLANES - shift, axis=0))
             for i in range(n)]
        for dist in (8, 4, 2, 1):
            for i in range(n):
                if i % (2 * dist) < dist:
                    j = i + dist
                    v[i], v[j] = jnp.maximum(v[i], v[j]), jnp.minimum(v[i], v[j])
    rows = [x[0:1, :] for x in v]
    for r in range(n):
        vals_ref[r:r + 1, :] = rows[r]
    count = jnp.sum(jnp.where(s >= rows[n - 1], 1.0, 0.0), axis=0, keepdims=True)
    for r in range(n - 1):
        count = count + jnp.where(rows[r] == rows[r + 1], 1.0, 0.0)
    return count


def _route_tile(s1, s2, v1_scr, v2_scr, ties):
    width = s1.shape[1]
    sub = lax.broadcasted_iota(jnp.int32, (SUBLANES, width), 0).astype(F32)
    if ties:
        rank1 = _top16_exact(s1, v1_scr)
        rank2 = _top16_exact(s2, v2_scr)
        marked1 = marked2 = jnp.full((1, width), float(PEER_TOPK), F32)
    else:
        marked1 = _top16_sorted(s1, v1_scr)
        marked2 = _top16_sorted(s2, v2_scr)
        rank2 = jnp.zeros(s2.shape, F32)
        for r in range(PEER_TOPK):
            rank2 = jnp.where(v2_scr[r:r + 1, :] > s2, float(r + 1), rank2)

    v2a = v2_scr[0:8, :]
    v2b = v2_scr[8:16, :]
    cands = [v1_scr[0:1, :] + v2a, v1_scr[0:1, :] + v2b]
    ids = [sub, sub + 8]
    for r in range(1, 8):
        cands.append(jnp.where(sub < CAND_COLS[r], v1_scr[r:r + 1, :] + v2a, NEG_INF))
        ids.append(sub + PEER_TOPK * r)
    cands.append(v1_scr[8:16, :] + v2_scr[0:1, :])
    ids.append((sub + 8) * PEER_TOPK)
    taken = [jnp.zeros((SUBLANES, width), F32) for _ in cands]
    mx = v1_scr[0:1, :] + v2_scr[0:1, :]
    zsum = jnp.zeros((1, width), F32)
    for _ in range(PEER_TOPK):
        m = functools.reduce(jnp.maximum, cands)
        m = jnp.max(m, axis=0, keepdims=True)
        sels = [c == m for c in cands]
        if ties:
            first = functools.reduce(
                jnp.minimum, [jnp.where(s, i, BIG_IDX) for s, i in zip(sels, ids)])
            first = jnp.min(first, axis=0, keepdims=True)
            sels = [i == first for i in ids]
        cands = [jnp.where(s, NEG_INF, c) for s, c in zip(sels, cands)]
        taken = [jnp.where(s, 1.0, t) for s, t in zip(sels, taken)]
        zsum = zsum + jnp.exp(m - mx)

    counts = [jnp.sum(taken[0] + taken[1], axis=0, keepdims=True)]
    for r in range(1, 8):
        counts.append(jnp.sum(taken[r + 1], axis=0, keepdims=True))
    for r in range(8, PEER_TOPK):
        counts.append(taken[9][r - 8:r - 7, :])
    n1 = jnp.zeros(s1.shape, F32)
    for r in range(PEER_TOPK):
        hit = (rank1 == r) if ties else (s1 == v1_scr[r:r + 1, :])
        n1 = jnp.where(hit, counts[r], n1)

    p1 = jnp.exp(s1 - v1_scr[0:1, :]) * (1.0 / zsum)
    p2 = jnp.exp(s2 - v2_scr[0:1, :])
    return n1, p1, rank2, p2, [marked1, marked2, functools.reduce(jnp.add, counts)]


def _route_kernel(h_ref, wq_ref, k1_ref, k2_ref, rank2_ref, p2_ref, n1_ref, p1_ref,
                  s1_scr, s2_scr, v1_scr, v2_scr):
    half = PEER_DQ // 2

    def scores(h):
        qt = _dot(wq_ref[h], h_ref[...])
        s1_scr[h] = _dot(k1_ref[h], qt[:half].astype(BF16))
        s2_scr[h] = _dot(k2_ref[h], qt[half:].astype(BF16))

    scores(0)

    def body(h, carry):
        s1 = s1_scr[h]
        s2 = s2_scr[h]
        scores(jnp.minimum(h + 1, PEER_HEADS - 1))

        def run(ties):
            n1, p1, rank2, p2, marked = _route_tile(s1, s2, v1_scr, v2_scr, ties)
            n1_ref[h] = n1
            p1_ref[h] = p1
            rank2_ref[h] = rank2.astype(rank2_ref.dtype)
            p2_ref[h] = p2.astype(p2_ref.dtype)
            return marked

        marked = run(ties=False)
        bad = functools.reduce(
            jnp.maximum, [jnp.where(mk == PEER_TOPK, 0.0, 1.0) for mk in marked])

        @pl.when(jnp.max(bad) > 0.0)
        def _():
            run(ties=True)

        return carry

    lax.fori_loop(0, PEER_HEADS, body, 0)


def _route(h2t, wq, k1, k2):
    n_t, d, tm = h2t.shape
    t = n_t * tm
    hk = (PEER_HEADS, N_KEYS, tm)
    tiled = (t // tm,) + hk
    out_spec = pl.BlockSpec((None,) + hk, lambda i: (i, 0, 0, 0))
    kspec = pl.BlockSpec((PEER_HEADS, N_KEYS, PEER_DQ // 2), lambda i: (0, 0, 0))
    wq_heads = wq.reshape(d, PEER_HEADS, PEER_DQ).transpose(1, 2, 0)
    return pl.pallas_call(
        _route_kernel,
        name="peer_route",
        grid=(t // tm,),
        in_specs=[
            pl.BlockSpec((None, d, tm), lambda i: (i, 0, 0)),
            pl.BlockSpec(wq_heads.shape, lambda i: (0, 0, 0)),
            kspec, kspec,
        ],
        out_specs=[out_spec, out_spec, out_spec, out_spec],
        out_shape=[
            jax.ShapeDtypeStruct(tiled, BF16),
            jax.ShapeDtypeStruct(tiled, BF16),
            jax.ShapeDtypeStruct(tiled, F32),
            jax.ShapeDtypeStruct(tiled, F32),
        ],
        scratch_shapes=[
            pltpu.VMEM(hk, F32), pltpu.VMEM(hk, F32),
            pltpu.VMEM((PEER_TOPK, tm), F32), pltpu.VMEM((PEER_TOPK, tm), F32),
        ],
        compiler_params=_cparams(("parallel",)),
    )(h2t, wq_heads, k1, k2)


def _peer_kernel(ht_ref, pu_ref, pvp_ref, pvc_ref, rank2_ref, p2_ref, n1_ref, p1_ref, o_ref,
                 p0_scr, p1_scr, *, a_rows, n_s):
    j = pl.program_id(1)
    o_rows = o_ref.shape[0] // a_rows

    def activations(sub, k, p_scr):
        scores = _dot(pu_ref[(sub * a_rows + k) * N_KEYS:(sub * a_rows + k + 1) * N_KEYS, :],
                      ht_ref[...])
        a = (2 * j + sub) * a_rows + k
        w = None
        for h in range(PEER_HEADS):
            n1 = n1_ref[h, pl.ds(a, 1), :].astype(BF16)
            p1 = p1_ref[h, pl.ds(a, 1), :].astype(BF16)
            term = jnp.where(rank2_ref[h] < n1, p2_ref[h], jnp.zeros((), BF16)) * p1
            w = term if w is None else w + term
        p_scr[k * N_KEYS:(k + 1) * N_KEYS, :] = (_gelu(scores) * w.astype(F32)).astype(BF16)

    def accumulate(k, pv_ref, p_scr):
        rows = slice(k * o_rows, (k + 1) * o_rows)
        o_ref[rows, :] += _dot(pv_ref[rows, :], p_scr[...])

    def second_half():
        for k in range(a_rows):
            activations(1, k, p1_scr)
            accumulate(k, pvc_ref, p0_scr)

    @pl.when(j == 0)
    def _():
        o_ref[...] = jnp.zeros(o_ref.shape, F32)
        for k in range(a_rows):
            activations(0, k, p0_scr)
        second_half()

    @pl.when((j > 0) & (j < n_s))
    def _():
        for k in range(a_rows):
            activations(0, k, p0_scr)
            accumulate(k, pvp_ref, p1_scr)
        second_half()

    @pl.when(j == n_s)
    def _():
        for k in range(a_rows):
            accumulate(k, pvp_ref, p1_scr)


PEER_TE = 512


def _peer_dense(ht, pu, pvt, rank2, p2, n1, p1):
    n_t, d, tm = ht.shape
    te = PEER_TE
    n_e = N_EXPERTS // te
    n_s = n_e // 2
    rspec = pl.BlockSpec((None, PEER_HEADS, N_KEYS, tm), lambda i, j: (i, 0, 0, 0))
    return pl.pallas_call(
        functools.partial(_peer_kernel, a_rows=te // N_KEYS, n_s=n_s),
        name="peer_dense",
        grid=(n_t, n_s + 1),
        in_specs=[
            pl.BlockSpec((None, d, tm), lambda i, j: (i, 0, 0)),
            pl.BlockSpec((2 * te, d), lambda i, j: (jnp.minimum(j, n_s - 1), 0)),
            pl.BlockSpec((None, d, te), lambda i, j: (jnp.clip(2 * j - 1, 0, n_e - 1), 0, 0)),
            pl.BlockSpec((None, d, te), lambda i, j: (jnp.minimum(2 * j, n_e - 1), 0, 0)),
            rspec, rspec, rspec, rspec,
        ],
        out_specs=pl.BlockSpec((None, d, tm), lambda i, j: (i, 0, 0)),
        out_shape=jax.ShapeDtypeStruct((n_t, d, tm), F32),
        scratch_shapes=[pltpu.VMEM((te, tm), BF16), pltpu.VMEM((te, tm), BF16)],
        compiler_params=_cparams(("parallel", "arbitrary")),
    )(ht, pu, pvt, pvt, rank2, p2, n1, p1)


def _resid_kernel(x_ref, yt_ref, gt_ref, g_ref, o_ref, *, final):
    x2 = x_ref[...] + gt_ref[...] * yt_ref[...].T
    if final:
        x2 = x2 * lax.rsqrt(jnp.mean(x2 * x2, axis=-1, keepdims=True) + EPS) * g_ref[...]
    o_ref[...] = x2


def _resid(x1, yt, gt2, final_g, seq, final):
    t, d = x1.shape
    tm = min(256, seq)
    per_b = seq // tm
    per_p = yt.shape[2] // tm
    return pl.pallas_call(
        functools.partial(_resid_kernel, final=final),
        name="peer_resid",
        grid=(t // tm,),
        in_specs=[
            pl.BlockSpec((tm, d), lambda i: (i, 0)),
            pl.BlockSpec((None, d, tm), lambda i: (i // per_p, 0, i % per_p)),
            pl.BlockSpec((None, 1, d), lambda i: (i // per_b, 0, 0)),
            pl.BlockSpec((1, d), lambda i: (0, 0)),
        ],
        out_specs=pl.BlockSpec((tm, d), lambda i: (i, 0)),
        out_shape=jax.ShapeDtypeStruct((t, d), F32),
        compiler_params=_cparams(("parallel",)),
    )(x1, yt, gt2, final_g)


def _layout_w_in(w_in_l):
    d = w_in_l.shape[0]
    o_a = 2 * GLA_QK + 2 * GLA_V
    o_u = o_a + GLA_RANK
    pad = jnp.zeros((d, PROJ_COLS - (w_in_l.shape[1] - GLA_RANK) - GLA_RANK), w_in_l.dtype)
    w = jnp.concatenate([w_in_l[:, :o_a], w_in_l[:, o_u:], w_in_l[:, o_a:o_u], pad], axis=1)
    w = w.astype(BF16).reshape(d, PROJ_COLS // PROJ_TN, PROJ_TN)
    return w.transpose(1, 0, 2)


def kernel(x, c, ada_w, ada_b, norm1_g, w_in, gla_w_a2, gla_b_a, gla_norm_g, gmlp_vnorm_g,
           gmlp_ws, gmlp_b, gmlp_out_g, w_out, norm2_g, peer_wq, peer_k1, peer_k2, peer_u,
           peer_v, final_g):
    batch, seq, d = x.shape
    depth = ada_w.shape[0]
    t = batch * seq
    xf = x.reshape(t, d)

    mod = _modulation(c, ada_w, ada_b)
    causal = jnp.tril(jnp.ones((GMLP_CHUNK, GMLP_CHUNK), F32))

    for l in range(depth):
        sh1, sc1, gt1, sh2, sc2, gt2 = [m.reshape(batch, 1, d) for m in jnp.split(mod[l], 6, axis=-1)]

        proj = _inproj(xf, norm1_g[l].reshape(1, d), sc1, sh1, _layout_w_in(w_in[l]), seq)

        wa2p = jnp.zeros((LANES, GLA_QK), F32).at[:GLA_RANK].set(gla_w_a2[l])
        y_gla = _gla(proj.reshape(batch, seq, PROJ_COLS), wa2p, gla_b_a[l].reshape(1, GLA_QK),
                     gla_norm_g[l].reshape(1, GLA_V), batch, seq).reshape(t, GLA_V)

        bias_full = jnp.repeat(gmlp_b[l].T, GMLP_DH, axis=1)
        x1, h2t = _outproj(
            y_gla, proj, (gmlp_ws[l] * causal).astype(BF16), bias_full,
            gmlp_vnorm_g[l].reshape(1, GMLP_WIDTH), gmlp_out_g[l].reshape(1, GMLP_WIDTH),
            w_out[l].astype(BF16), xf, gt1, norm2_g[l].reshape(1, d), sc2, sh2, seq)

        rank2, p2, n1, p1 = _route(h2t, peer_wq[l].astype(BF16), peer_k1[l].astype(BF16),
                                   peer_k2[l].astype(BF16))
        pvt = peer_v[l].reshape(N_EXPERTS // PEER_TE, PEER_TE, d).transpose(0, 2, 1).astype(BF16)
        yt = _peer_dense(h2t, peer_u[l].astype(BF16), pvt, rank2, p2, n1, p1)
        xf = _resid(x1, yt, gt2, final_g.reshape(1, d), seq, final=(l == depth - 1))

    return xf.reshape(batch, seq, d)
```
